```python
import math
import jax, jax.numpy as jnp
from jax import lax
import numpy as np

D_MODEL = 4096
BATCH = 2
SEQ = 8192
DEPTH = 4

CHUNK = 64
N_A_LAYERS = DEPTH // 2
N_B_LAYERS = DEPTH - N_A_LAYERS
EPS = 1e-6

SSM_EXPAND = 2
SSM_D_INNER = SSM_EXPAND * D_MODEL
SSM_HEADDIM = 64
SSM_HEADS = SSM_D_INNER // SSM_HEADDIM
SSM_GROUPS = 8
SSM_HEADS_PER_GROUP = SSM_HEADS // SSM_GROUPS
SSM_STATE = 128
SSM_CONV = 4
SSM_SCAN_CHUNK = CHUNK
SSM_CONV_DIM = SSM_D_INNER + 2 * SSM_GROUPS * SSM_STATE
SSM_IN_DIM = SSM_D_INNER + SSM_CONV_DIM + SSM_HEADS

ATT_HEADDIM = 128
ATT_HEADS = D_MODEL // ATT_HEADDIM
ATT_D = ATT_HEADS * ATT_HEADDIM
ATT_LEFT_CHUNKS = 8
ATT_BAND = (ATT_LEFT_CHUNKS + 1) * CHUNK
REL_MAX = 128
REL_BUCKETS = 2 * REL_MAX + 1

kernel_name = "yoco_mamba2_chunked_relattn_trunk"


def rms_norm(x, g):
    xf = x.astype(jnp.float32)
    var = jnp.mean(xf * xf, axis=-1, keepdims=True)
    return (xf * lax.rsqrt(var + EPS)).astype(x.dtype) * g


def ada_modulation(c, w, b):
    mod = jax.nn.silu(c) @ w + b
    shift, scale, gate = jnp.split(mod, 3, axis=-1)
    return shift[:, None], scale[:, None], gate[:, None]


def causal_depthwise_conv(u, w, b):
    K = w.shape[0]
    S = u.shape[1]
    up = jnp.pad(u, ((0, 0), (K - 1, 0), (0, 0)))
    return b + sum(up[:, k:k + S] * w[k] for k in range(K))


def ssd_chunked_scan(xh, dt, A, Bm, Cm):
    Bsz, S = xh.shape[:2]
    L = SSM_SCAN_CHUNK
    nc = S // L
    f32 = jnp.float32

    def to_chunks(t):
        return jnp.moveaxis(t.astype(f32).reshape((Bsz, nc, L) + t.shape[2:]), 1, 0)

    xs = (to_chunks(xh), to_chunks(dt), to_chunks(Bm), to_chunks(Cm))
    causal = jnp.tril(jnp.ones((L, L), dtype=bool))[None, :, :, None, None]
    Af = A.astype(f32)

    def step(state, inp):
        xc, dtc, bc, cc = inp
        a_cum = jnp.cumsum(dtc * Af, axis=1)
        seg = a_cum[:, :, None] - a_cum[:, None, :]
        decay = jnp.exp(jnp.where(causal, seg, -jnp.inf))
        cb = jnp.einsum('blgn,bsgn->blsg', cc, bc)
        wts = cb[..., None] * decay * dtc[:, None]
        y_intra = jnp.einsum('blsgr,bsgrp->blgrp', wts, xc)
        y_inter = jnp.einsum('blgn,bgrpn->blgrp', cc, state) * jnp.exp(a_cum)[..., None]
        to_end = jnp.exp(a_cum[:, -1:] - a_cum) * dtc
        new_state = (state * jnp.exp(a_cum[:, -1])[..., None, None]
                     + jnp.einsum('bsgr,bsgrp,bsgn->bgrpn', to_end, xc, bc))
        return new_state, y_intra + y_inter

    state0 = jnp.zeros((Bsz, SSM_GROUPS, SSM_HEADS_PER_GROUP, SSM_HEADDIM, SSM_STATE), f32)
    _, ys = lax.scan(step, state0, xs)
    y = jnp.moveaxis(ys, 0, 1).reshape(xh.shape)
    return y.astype(xh.dtype)


def mamba2_mixer(h, w_in, conv_w, conv_b, dt_bias, a_log, d_skip, norm_g, w_out):
    Bsz, S, _ = h.shape
    G, R, P, N = SSM_GROUPS, SSM_HEADS_PER_GROUP, SSM_HEADDIM, SSM_STATE
    zxbcdt = h @ w_in
    z, xbc, dt = jnp.split(zxbcdt, [SSM_D_INNER, SSM_D_INNER + SSM_CONV_DIM], axis=-1)
    xbc = jax.nn.silu(causal_depthwise_conv(xbc, conv_w, conv_b))
    xs, Bm, Cm = jnp.split(xbc, [SSM_D_INNER, SSM_D_INNER + G * N], axis=-1)
    xh = xs.reshape(Bsz, S, G, R, P)
    Bm = Bm.reshape(Bsz, S, G, N)
    Cm = Cm.reshape(Bsz, S, G, N)
    dt = jax.nn.softplus((dt + dt_bias).astype(jnp.float32)).reshape(Bsz, S, G, R)
    A = -jnp.exp(a_log.astype(jnp.float32)).reshape(G, R)
    y = ssd_chunked_scan(xh, dt, A, Bm, Cm)
    y = y + xh * d_skip.reshape(G, R)[:, :, None]
    yg = (y.reshape(Bsz, S, G, R * P) * jax.nn.silu(z).reshape(Bsz, S, G, R * P))
    y = rms_norm(yg, norm_g).reshape(Bsz, S, SSM_D_INNER)
    return y @ w_out


def rel_position_bias(rel_table):
    l = jnp.arange(CHUNK)[:, None]
    j = jnp.arange(ATT_BAND)[None, :]
    dist = l + ATT_LEFT_CHUNKS * CHUNK - j
    bucket = jnp.clip(dist, -REL_MAX, REL_MAX) + REL_MAX
    return rel_table[:, bucket]


def chunked_rel_attention(q, k, v, rel_table):
    Bsz, S = q.shape[:2]
    nc = S // CHUNK
    pad = ATT_LEFT_CHUNKS * CHUNK
    kp = jnp.pad(k, ((0, 0), (pad, 0), (0, 0), (0, 0)))
    vp = jnp.pad(v, ((0, 0), (pad, 0), (0, 0), (0, 0)))
    qc = jnp.moveaxis(q.reshape(Bsz, nc, CHUNK, ATT_HEADS, ATT_HEADDIM), 1, 0)
    bias = rel_position_bias(rel_table).astype(jnp.float32)[None]
    scale = ATT_HEADDIM ** -0.5
    key_slot = jnp.arange(ATT_BAND)

    def one_chunk(args):
        i, qi = args
        start = i * CHUNK
        kb = lax.dynamic_slice_in_dim(kp, start, ATT_BAND, axis=1)
        vb = lax.dynamic_slice_in_dim(vp, start, ATT_BAND, axis=1)
        s = jnp.einsum('blhd,bjhd->bhlj', qi, kb).astype(jnp.float32) * scale + bias
        valid = key_slot >= pad - start
        s = jnp.where(valid, s, -jnp.inf)
        p = jax.nn.softmax(s, axis=-1)
        return jnp.einsum('bhlj,bjhd->blhd', p.astype(vb.dtype), vb)

    o = lax.map(one_chunk, (jnp.arange(nc), qc))
    return jnp.moveaxis(o, 0, 1).reshape(Bsz, S, ATT_D)


def chunked_attention_mixer(h, k, v, w_in, rel_table, w_out):
    Bsz, S, _ = h.shape
    q, z = jnp.split(h @ w_in, 2, axis=-1)
    o = chunked_rel_attention(q.reshape(Bsz, S, ATT_HEADS, ATT_HEADDIM), k, v, rel_table)
    return (o * jax.nn.silu(z)) @ w_out


def setup_inputs(seed: int = 0) -> dict:
    key = jax.random.key(seed)
    ks = jax.random.split(key, 20)
    f32 = jnp.float32

    def normal(k, shape, scale):
        return jax.random.normal(k, shape, f32) * scale

    x = normal(ks[0], (BATCH, SEQ, D_MODEL), 1.0)
    c = normal(ks[1], (BATCH, D_MODEL), 1.0)
    ada_w = normal(ks[2], (DEPTH, D_MODEL, 3 * D_MODEL), 0.5 * D_MODEL ** -0.5)
    ada_b = normal(ks[3], (DEPTH, 3 * D_MODEL), 0.01)
    pre_norm_g = 1.0 + normal(ks[4], (DEPTH, D_MODEL), 0.05)
    post_norm_g = 1.0 + normal(ks[5], (DEPTH, D_MODEL), 0.05)
    ssm_w_in = normal(ks[6], (N_A_LAYERS, D_MODEL, SSM_IN_DIM), D_MODEL ** -0.5)
    ssm_conv_w = normal(ks[7], (N_A_LAYERS, SSM_CONV, SSM_CONV_DIM), SSM_CONV ** -0.5)
    ssm_conv_b = normal(ks[8], (N_A_LAYERS, SSM_CONV_DIM), 0.01)
    dt0 = jnp.exp(jax.random.uniform(ks[9], (N_A_LAYERS, SSM_HEADS), f32,
                                     math.log(1e-3), math.log(1e-1)))
    ssm_dt_bias = dt0 + jnp.log(-jnp.expm1(-dt0))
    ssm_a_log = jnp.log(jax.random.uniform(ks[10], (N_A_LAYERS, SSM_HEADS), f32, 1.0, 16.0))
    ssm_d = 1.0 + normal(ks[11], (N_A_LAYERS, SSM_HEADS), 0.1)
    ssm_norm_g = 1.0 + normal(ks[12], (N_A_LAYERS, SSM_GROUPS, SSM_D_INNER // SSM_GROUPS), 0.05)
    ssm_w_out = normal(ks[13], (N_A_LAYERS, SSM_D_INNER, D_MODEL), SSM_D_INNER ** -0.5)
    kv_norm_g = 1.0 + normal(ks[14], (D_MODEL,), 0.05)
    w_kv = normal(ks[15], (D_MODEL, 2 * ATT_D), D_MODEL ** -0.5)
    att_w_in = normal(ks[16], (N_B_LAYERS, D_MODEL, 2 * ATT_D), D_MODEL ** -0.5)
    att_rel_bias = normal(ks[17], (N_B_LAYERS, ATT_HEADS, REL_BUCKETS), 0.5)
    att_w_out = normal(ks[18], (N_B_LAYERS, ATT_D, D_MODEL), ATT_D ** -0.5)
    return {"x": x, "c": c, "ada_w": ada_w, "ada_b": ada_b,
            "pre_norm_g": pre_norm_g, "post_norm_g": post_norm_g,
            "ssm_w_in": ssm_w_in, "ssm_conv_w": ssm_conv_w, "ssm_conv_b": ssm_conv_b,
            "ssm_dt_bias": ssm_dt_bias, "ssm_a_log": ssm_a_log, "ssm_d": ssm_d,
            "ssm_norm_g": ssm_norm_g, "ssm_w_out": ssm_w_out,
            "kv_norm_g": kv_norm_g, "w_kv": w_kv,
            "att_w_in": att_w_in, "att_rel_bias": att_rel_bias, "att_w_out": att_w_out}


def reference(x, c, ada_w, ada_b, pre_norm_g, post_norm_g, ssm_w_in, ssm_conv_w, ssm_conv_b,
              ssm_dt_bias, ssm_a_log, ssm_d, ssm_norm_g, ssm_w_out, kv_norm_g, w_kv,
              att_w_in, att_rel_bias, att_w_out):
    Bsz, S, _ = x.shape
    k = None
    v = None
    for layer in range(DEPTH):
        shift, scale, gate = ada_modulation(c, ada_w[layer], ada_b[layer])
        h = rms_norm(x, pre_norm_g[layer]) * (1.0 + scale) + shift
        if layer < N_A_LAYERS:
            i = layer
            y = mamba2_mixer(h, ssm_w_in[i], ssm_conv_w[i], ssm_conv_b[i], ssm_dt_bias[i],
                             ssm_a_log[i], ssm_d[i], ssm_norm_g[i], ssm_w_out[i])
        else:
            i = layer - N_A_LAYERS
            if i == 0:
                kv = rms_norm(x, kv_norm_g) @ w_kv
                k, v = jnp.split(kv, 2, axis=-1)
                k = k.reshape(Bsz, S, ATT_HEADS, ATT_HEADDIM)
                v = v.reshape(Bsz, S, ATT_HEADS, ATT_HEADDIM)
            y = chunked_attention_mixer(h, k, v, att_w_in[i], att_rel_bias[i], att_w_out[i])
        x = x + gate * rms_norm(y, post_norm_g[layer])
    return x
```

```python
import functools

import jax
import jax.numpy as jnp
import numpy as np
from jax import lax
from jax.experimental import pallas as pl
from jax.experimental.pallas import tpu as pltpu

F32 = jnp.float32
BF16 = jnp.bfloat16

EPS = 1e-6
CHUNK = 64
LANES = 128
V7X_VMEM_BYTES = 64 * 1024 * 1024
VMEM_LIMIT = V7X_VMEM_BYTES - 8 * 1024 * 1024

SSM_HEADDIM = 64
SSM_GROUPS = 8
SSM_STATE = 128
SSM_CONV = 4
ATT_HEADDIM = 128
ATT_LEFT_CHUNKS = 8
REL_MAX = 128
ATT_QBLK = 4 * CHUNK
ATT_KBLK = 3 * ATT_QBLK
NEG = -1e30


def _params(n_grid, vmem=VMEM_LIMIT):
    return pltpu.CompilerParams(dimension_semantics=("arbitrary",) * n_grid, vmem_limit_bytes=vmem)


def _silu(v):
    return v * jax.nn.sigmoid(v)


def _softplus(v):
    return jnp.maximum(v, 0.0) + jnp.log1p(jnp.exp(-jnp.abs(v)))


def _ada_kernel(ct_ref, w_ref, b_ref, o_ref, sb_ref, *, nb, tn):
    d = ct_ref.shape[0]

    @pl.when((pl.program_id(0) == 0) & (pl.program_id(1) == 0))
    def _():
        s = _silu(ct_ref[...])
        for b in range(nb):
            sb_ref[b] = jnp.broadcast_to(s[:, b:b + 1], (d, LANES))

    rows = []
    for b in range(nb):
        cols = []
        for t in range(tn // LANES):
            w = w_ref[0, :, t * LANES:(t + 1) * LANES]
            cols.append(jnp.sum(w * sb_ref[b], axis=0, keepdims=True))
        rows.append(jnp.concatenate(cols, axis=1))
    o_ref[0] = jnp.concatenate(rows, axis=0) + b_ref[0]


def _ada_modulation(c, ada_w, ada_b, tn=512):
    depth, d, n3 = ada_w.shape
    nb = c.shape[0]
    return pl.pallas_call(
        functools.partial(_ada_kernel, nb=nb, tn=tn),
        grid=(depth, n3 // tn),
        in_specs=[pl.BlockSpec((d, nb), lambda l, j: (0, 0)),
                  pl.BlockSpec((1, d, tn), lambda l, j: (l, 0, j)),
                  pl.BlockSpec((1, 1, tn), lambda l, j: (l, 0, j))],
        out_specs=pl.BlockSpec((1, nb, tn), lambda l, j: (l, 0, j)),
        out_shape=jax.ShapeDtypeStruct((depth, nb, n3), F32),
        scratch_shapes=[pltpu.VMEM((nb, d, LANES), F32)],
        compiler_params=_params(2),
        name="ada_modulation",
    )(c.T, ada_w, ada_b.reshape(depth, 1, n3))


def _rms(v, g):
    var = jnp.mean(v * v, axis=-1, keepdims=True)
    return v * lax.rsqrt(var + EPS) * g


def _resnorm_kernel(*refs, has_post, has_pre, has_kv):
    it = iter(refs)
    x_ref = next(it)
    if has_post:
        y_ref, gate_ref, gpost_ref = next(it), next(it), next(it)
    if has_pre:
        shift_ref, scale_ref, gpre_ref = next(it), next(it), next(it)
    if has_kv:
        gkv_ref = next(it)
    x = x_ref[0]
    if has_post:
        x = x + gate_ref[0] * _rms(y_ref[0].astype(F32), gpost_ref[...])
        next(it)[0] = x
    if has_pre:
        h = _rms(x, gpre_ref[...]) * (1.0 + scale_ref[0]) + shift_ref[0]
        next(it)[0] = h.astype(BF16)
    if has_kv:
        next(it)[0] = _rms(x, gkv_ref[...]).astype(BF16)


def _resnorm(x, post=None, pre=None, kv_g=None, tr=256):
    bsz, s, d = x.shape
    row = pl.BlockSpec((1, tr, d), lambda b, i: (b, i, 0))
    per_b = pl.BlockSpec((1, 1, d), lambda b, i: (b, 0, 0))
    vec = pl.BlockSpec((1, d), lambda b, i: (0, 0))
    args, in_specs, out_specs, out_shape = [x], [row], [], []
    if post is not None:
        y, gate, g_post = post
        args += [y, gate, g_post.reshape(1, d)]
        in_specs += [row, per_b, vec]
        out_specs.append(row)
        out_shape.append(jax.ShapeDtypeStruct((bsz, s, d), F32))
    if pre is not None:
        shift, scale, g_pre = pre
        args += [shift, scale, g_pre.reshape(1, d)]
        in_specs += [per_b, per_b, vec]
        out_specs.append(row)
        out_shape.append(jax.ShapeDtypeStruct((bsz, s, d), BF16))
    if kv_g is not None:
        args.append(kv_g.reshape(1, d))
        in_specs.append(vec)
        out_specs.append(row)
        out_shape.append(jax.ShapeDtypeStruct((bsz, s, d), BF16))
    return pl.pallas_call(
        functools.partial(_resnorm_kernel, has_post=post is not None, has_pre=pre is not None,
                          has_kv=kv_g is not None),
        grid=(bsz, s // tr), in_specs=in_specs, out_specs=out_specs, out_shape=out_shape,
        compiler_params=_params(2), name="resnorm",
    )(*args)


def _mm_kernel(a_ref, b_ref, o_ref):
    o_ref[...] = jnp.dot(a_ref[...], b_ref[...], preferred_element_type=F32).astype(o_ref.dtype)


def _matmul(a, b, out_dtype, tm, tn, name):
    m, k = a.shape
    n = b.shape[1]
    tm, tn = min(tm, m), min(tn, n)
    return pl.pallas_call(
        _mm_kernel, grid=(m // tm, n // tn),
        in_specs=[pl.BlockSpec((tm, k), lambda i, j: (i, 0)),
                  pl.BlockSpec((k, tn), lambda i, j: (0, j))],
        out_specs=pl.BlockSpec((tm, tn), lambda i, j: (i, j)),
        out_shape=jax.ShapeDtypeStruct((m, n), out_dtype),
        compiler_params=_params(2), name=name,
    )(a, b)


CONV_HALO = 16


def _conv_kernel(u_ref, prev_ref, w_ref, b_ref, o_ref):
    ts = u_ref.shape[1]
    cur = u_ref[0].astype(F32)
    tail = prev_ref[0].astype(F32)
    tail = jnp.where(pl.program_id(1) == 0, 0.0, tail)
    ext = jnp.concatenate([tail, cur], axis=0)
    w = w_ref[...]
    acc = b_ref[...] + w[SSM_CONV - 1:SSM_CONV] * cur
    for k in range(SSM_CONV - 1):
        off = CONV_HALO - (SSM_CONV - 1) + k
        acc = acc + w[k:k + 1] * ext[off:off + ts]
    o_ref[0] = _silu(acc).astype(o_ref.dtype)


def _conv_silu(zx, conv_w, conv_b, col0, ts=512, tc=1024):
    bsz, s, _ = zx.shape
    c = conv_w.shape[1]
    cb0 = col0 // tc
    hb = ts // CONV_HALO
    return pl.pallas_call(
        _conv_kernel, grid=(bsz, s // ts, c // tc),
        in_specs=[pl.BlockSpec((1, ts, tc), lambda b, i, j: (b, i, cb0 + j)),
                  pl.BlockSpec((1, CONV_HALO, tc), lambda b, i, j: (b, jnp.maximum(i * hb - 1, 0), cb0 + j)),
                  pl.BlockSpec((SSM_CONV, tc), lambda b, i, j: (0, j)),
                  pl.BlockSpec((1, tc), lambda b, i, j: (0, j))],
        out_specs=pl.BlockSpec((1, ts, tc), lambda b, i, j: (b, i, j)),
        out_shape=jax.ShapeDtypeStruct((bsz, s, c), BF16),
        compiler_params=_params(3), name="conv_silu",
    )(zx, zx, conv_w, conv_b.reshape(1, c))


def _scan_kernel(x_ref, b_ref, c_ref, z_ref, dta_ref, dtt_ref, alr_ref, alc_ref, dbr_ref, dbc_ref,
                 dexp_ref, ng_ref, o_ref, st_ref, *, nsub):
    L = CHUNK
    P = SSM_HEADDIM
    hp = LANES // P
    nheads = alr_ref.shape[2]
    npair = nheads // hp

    @pl.when(pl.program_id(2) == 0)
    def _():
        st_ref[...] = jnp.zeros_like(st_ref)

    a_r = -jnp.exp(alr_ref[0])
    a_c = -jnp.exp(alc_ref[0])
    lane = lax.broadcasted_iota(jnp.int32, (L, LANES), 1)
    sub = lax.broadcasted_iota(jnp.int32, (L, LANES), 0)
    lo = lane < P
    s_of_lane = lane & (P - 1)
    causal2 = s_of_lane <= sub
    tri = (lax.broadcasted_iota(jnp.int32, (L, L), 1) <= lax.broadcasted_iota(jnp.int32, (L, L), 0)).astype(F32)
    tri_t2 = (sub <= s_of_lane).astype(F32)
    blk = lax.broadcasted_iota(jnp.int32, (L, 2 * LANES), 1) // P
    dexp = dexp_ref[...]
    ng = ng_ref[...]

    def chunk(ci, carry):
        rows = pl.ds(pl.multiple_of(ci * L, L), L)
        x = x_ref[0, rows, :]
        bm = b_ref[0, rows, :]
        cm = c_ref[0, rows, :]
        dt = _softplus(dta_ref[0, ci, 0] + dbr_ref[0])
        dt_t = _softplus(dtt_ref[0, ci, 0] + dbc_ref[0])
        a = dt * a_r
        acum = jnp.dot(tri, a, preferred_element_type=F32, precision=lax.Precision.HIGHEST)
        acum_t = jnp.dot(dt_t[:, :L] * a_c, tri_t2, preferred_element_type=F32,
                         precision=lax.Precision.HIGHEST)
        e_in = jnp.exp(acum)
        e_out = jnp.exp(acum[L - 1:L, :] - acum) * dt
        cb2 = lax.dot_general(cm, jnp.concatenate([bm, bm], axis=0), (((1,), (1,)), ((), ())),
                              preferred_element_type=F32)
        xf = x.astype(F32)
        y_in = jnp.dot(cm, st_ref[...].astype(BF16), preferred_element_type=F32)

        w_tiles, s_in, s_out = [], [], []
        for j in range(npair):
            r0, r1 = hp * j, hp * j + 1
            col = jnp.where(lo, acum[:, r0:r0 + 1], acum[:, r1:r1 + 1])
            row = jnp.where(lo[0:1], acum_t[r0:r0 + 1], acum_t[r1:r1 + 1])
            dtr = jnp.where(lo[0:1], dt_t[r0:r0 + 1], dt_t[r1:r1 + 1])
            seg = jnp.where(causal2, col - row, NEG)
            w_tiles.append((jnp.exp(seg) * (cb2 * dtr)).astype(BF16))
            s_in.append(jnp.where(lo, e_in[:, r0:r0 + 1], e_in[:, r1:r1 + 1]))
            s_out.append(jnp.where(lo, e_out[:, r0:r0 + 1], e_out[:, r1:r1 + 1]))
        sc_in = jnp.concatenate(s_in, axis=1)
        sc_out = jnp.concatenate(s_out, axis=1)

        y_parts = []
        for q in range(npair // 2):
            w4 = jnp.concatenate([w_tiles[2 * q], w_tiles[2 * q + 1]], axis=1)
            x4 = x[:, q * 2 * LANES:(q + 1) * 2 * LANES]
            xbd = jnp.concatenate([jnp.where(blk == i, x4, jnp.zeros_like(x4)) for i in range(2 * hp)], axis=0)
            y_parts.append(jnp.dot(w4, xbd, preferred_element_type=F32))
        y = jnp.concatenate(y_parts, axis=1) + y_in * sc_in + xf * dexp

        xw = (xf * sc_out).astype(BF16)
        upd = lax.dot_general(bm, xw, (((0,), (0,)), ((), ())), preferred_element_type=F32)
        st_ref[...] = st_ref[...] * sc_in[L - 1:L, :] + upd

        yg = y * _silu(z_ref[0, rows, :].astype(F32))
        o_ref[0, rows, :] = _rms(yg, ng).astype(o_ref.dtype)
        return carry

    lax.fori_loop(0, nsub, chunk, 0)


def _ssd_scan(xbc, zx, dt_raw, dt_bias, a_log, d_skip, norm_g, nsub=4):
    bsz, s, _ = xbc.shape
    g = SSM_GROUPS
    nheads = dt_raw.shape[-1]
    r = nheads // g
    gw = r * SSM_HEADDIM
    d_inner = nheads * SSM_HEADDIM
    nc = s // CHUNK
    lb = nsub * CHUNK
    dt5 = dt_raw.reshape(bsz, nc, CHUNK, g, r)
    dt_a = dt5.transpose(0, 1, 3, 2, 4)
    dt_t = dt5.transpose(0, 1, 3, 4, 2)
    dt_t = jnp.concatenate([dt_t, dt_t], axis=-1)
    nb_off = d_inner // SSM_STATE
    per_g_row = lambda arr: arr.reshape(g, 1, r)
    per_g_col = lambda arr: arr.reshape(g, r, 1)
    row_spec = pl.BlockSpec((1, 1, r), lambda b, gi, ci: (gi, 0, 0))
    col_spec = pl.BlockSpec((1, r, 1), lambda b, gi, ci: (gi, 0, 0))
    wide = pl.BlockSpec((1, lb, gw), lambda b, gi, ci: (b, ci, gi))
    return pl.pallas_call(
        functools.partial(_scan_kernel, nsub=nsub),
        grid=(bsz, g, nc // nsub),
        in_specs=[wide,
                  pl.BlockSpec((1, lb, SSM_STATE), lambda b, gi, ci: (b, ci, nb_off + gi)),
                  pl.BlockSpec((1, lb, SSM_STATE), lambda b, gi, ci: (b, ci, nb_off + g + gi)),
                  wide,
                  pl.BlockSpec((1, nsub, 1, CHUNK, r), lambda b, gi, ci: (b, ci, gi, 0, 0)),
                  pl.BlockSpec((1, nsub, 1, r, 2 * CHUNK), lambda b, gi, ci: (b, ci, gi, 0, 0)),
                  row_spec, col_spec, row_spec, col_spec,
                  pl.BlockSpec((1, gw), lambda b, gi, ci: (0, gi)),
                  pl.BlockSpec((1, gw), lambda b, gi, ci: (0, gi))],
        out_specs=wide,
        out_shape=jax.ShapeDtypeStruct((bsz, s, d_inner), BF16),
        scratch_shapes=[pltpu.VMEM((SSM_STATE, gw), F32)],
        compiler_params=_params(3), name="ssd_scan",
    )(xbc, xbc, xbc, zx, dt_a, dt_t, per_g_row(a_log), per_g_col(a_log), per_g_row(dt_bias),
      per_g_col(dt_bias), jnp.repeat(d_skip, SSM_HEADDIM).reshape(1, d_inner), norm_g.reshape(1, d_inner))


def _attn_kernel(q_ref, z_ref, k0_ref, k1_ref, k2_ref, v0_ref, v1_ref, v2_ref, bias_ref, o_ref, *, hb):
    i = pl.program_id(2)
    kpos = lax.broadcasted_iota(jnp.int32, (ATT_QBLK, ATT_KBLK), 1)
    valid = kpos >= (ATT_KBLK - ATT_QBLK) - ATT_QBLK * i
    for h in range(hb):
        sl = slice(h * ATT_HEADDIM, (h + 1) * ATT_HEADDIM)
        kc = jnp.concatenate([k0_ref[0, :, sl], k1_ref[0, :, sl], k2_ref[0, :, sl]], axis=0)
        vc = jnp.concatenate([v0_ref[0, :, sl], v1_ref[0, :, sl], v2_ref[0, :, sl]], axis=0)
        s = lax.dot_general(q_ref[0, :, sl], kc, (((1,), (1,)), ((), ())), preferred_element_type=F32)
        s = jnp.where(valid, s + bias_ref[h], NEG)
        p = jnp.exp(s - jnp.max(s, axis=1, keepdims=True))
        den = jnp.sum(p, axis=1, keepdims=True)
        o = jnp.dot(p.astype(BF16), vc, preferred_element_type=F32) / den
        o_ref[0, :, sl] = (o * _silu(z_ref[0, :, sl].astype(F32))).astype(o_ref.dtype)


def _rel_bias_tile(rel_table):
    l = np.arange(ATT_QBLK)[:, None]
    j = np.arange(ATT_KBLK)[None, :]
    dist = l + (ATT_KBLK - ATT_QBLK) - j
    bucket = np.clip(dist, -REL_MAX, REL_MAX) + REL_MAX
    lc, jc = l // CHUNK, j // CHUNK
    band = (jc >= lc) & (jc <= lc + ATT_LEFT_CHUNKS)
    return jnp.where(jnp.asarray(band)[None], rel_table[:, bucket].astype(F32), NEG)


def _attention(qz, kv, bias, hb=4):
    bsz, s, two_d = qz.shape
    d = two_d // 2
    hw = hb * ATT_HEADDIM
    ng = d // hw
    cur = lambda off: pl.BlockSpec((1, ATT_QBLK, hw), lambda hg, b, i: (b, i, off + hg))
    back = lambda off, n: pl.BlockSpec((1, ATT_QBLK, hw), lambda hg, b, i: (b, jnp.maximum(i - n, 0), off + hg))
    return pl.pallas_call(
        functools.partial(_attn_kernel, hb=hb),
        grid=(ng, bsz, s // ATT_QBLK),
        in_specs=[cur(0), cur(ng), back(0, 2), back(0, 1), cur(0), back(ng, 2), back(ng, 1), cur(ng),
                  pl.BlockSpec((hb, ATT_QBLK, ATT_KBLK), lambda hg, b, i: (hg, 0, 0))],
        out_specs=pl.BlockSpec((1, ATT_QBLK, hw), lambda hg, b, i: (b, i, hg)),
        out_shape=jax.ShapeDtypeStruct((bsz, s, d), BF16),
        compiler_params=_params(3), name="band_attention",
    )(qz, qz, kv, kv, kv, kv, kv, kv, bias)


def kernel(x, c, ada_w, ada_b, pre_norm_g, post_norm_g, ssm_w_in, ssm_conv_w, ssm_conv_b, ssm_dt_bias, ssm_a_log, ssm_d, ssm_norm_g, ssm_w_out, kv_norm_g, w_kv, att_w_in, att_rel_bias, att_w_out):
    bsz, s, d = x.shape
    depth = ada_w.shape[0]
    n_a = ssm_w_in.shape[0]
    t = bsz * s
    d_inner = ssm_w_out.shape[1]
    nheads = ssm_dt_bias.shape[1]
    zx_cols = ssm_w_in.shape[2] - nheads

    mod = _ada_modulation(c, ada_w, ada_b)
    shift = [mod[l, :, None, 0:d] for l in range(depth)]
    scale = [mod[l, :, None, d:2 * d] for l in range(depth)]
    gate = [mod[l, :, None, 2 * d:3 * d] for l in range(depth)]

    (h,) = _resnorm(x, pre=(shift[0], scale[0], pre_norm_g[0]))
    kv = None
    for layer in range(depth):
        h2 = h.reshape(t, d)
        if layer < n_a:
            i = layer
            w_in = ssm_w_in[i]
            zx = _matmul(h2, w_in[:, :zx_cols].astype(BF16), BF16, 1024, 1024, "ssm_in_proj")
            dt_raw = _matmul(h2, w_in[:, zx_cols:].astype(BF16), F32, 1024, LANES, "ssm_dt_proj")
            zx = zx.reshape(bsz, s, zx_cols)
            xbc = _conv_silu(zx, ssm_conv_w[i], ssm_conv_b[i], col0=d_inner)
            yn = _ssd_scan(xbc, zx, dt_raw.reshape(bsz, s, nheads), ssm_dt_bias[i], ssm_a_log[i],
                           ssm_d[i], ssm_norm_g[i])
            y = _matmul(yn.reshape(t, d_inner), ssm_w_out[i].astype(BF16), F32, 1024, 512, "ssm_out_proj")
        else:
            i = layer - n_a
            att_d = att_w_out.shape[1]
            qscale = jnp.concatenate([jnp.full((att_d,), ATT_HEADDIM ** -0.5, F32), jnp.ones((att_d,), F32)])
            qz = _matmul(h2, (att_w_in[i] * qscale).astype(BF16), BF16, 1024, 1024, "att_in_proj")
            o = _attention(qz.reshape(bsz, s, 2 * att_d), kv, _rel_bias_tile(att_rel_bias[i]))
            y = _matmul(o.reshape(t, att_d), att_w_out[i].astype(BF16), F32, 1024, 1024, "att_out_proj")
        post = (y.reshape(bsz, s, d), gate[layer], post_norm_g[layer])
        if layer + 1 < depth:
            nxt = layer + 1
            pre = (shift[nxt], scale[nxt], pre_norm_g[nxt])
            if nxt == n_a:
                x, h, hkv = _resnorm(x, post=post, pre=pre, kv_g=kv_norm_g)
                kv = _matmul(hkv.reshape(t, d), w_kv.astype(BF16), BF16, 1024, 1024, "kv_proj")
                kv = kv.reshape(bsz, s, w_kv.shape[1])
            else:
                x, h = _resnorm(x, post=post, pre=pre)
        else:
            (x,) = _resnorm(x, post=post)
    return x
```

```python
import functools

import jax
import jax.numpy as jnp
import numpy as np
from jax import lax
from jax.experimental import pallas as pl
from jax.experimental.pallas import tpu as pltpu

F32 = jnp.float32
BF16 = jnp.bfloat16

EPS = 1e-6
CHUNK = 64
LANES = 128
MXU_DIM = 256
V7X_VMEM_BYTES = 64 * 1024 * 1024
VMEM_LIMIT = V7X_VMEM_BYTES - 8 * 1024 * 1024

SSM_HEADDIM = 64
SSM_GROUPS = 8
SSM_STATE = 128
SSM_CONV = 4
ATT_HEADDIM = 128
ATT_LEFT_CHUNKS = 8
REL_MAX = 128
ATT_QBLK = 4 * CHUNK
ATT_KBLK = 3 * ATT_QBLK
ATT_HALF = ATT_QBLK // 2
ATT_WIN = ATT_KBLK - ATT_HALF
ATT_RVEC = 1024
ATT_VARIANTS = 3
LOG2E = 1.4426950408889634
NEG = -1e30


def _params(n_grid, vmem=VMEM_LIMIT):
    return pltpu.CompilerParams(dimension_semantics=("arbitrary",) * n_grid, vmem_limit_bytes=vmem)


def _silu(v):
    return v * jax.nn.sigmoid(v)


def _softplus(v):
    return jnp.maximum(v, 0.0) + jnp.log1p(jnp.exp(-jnp.abs(v)))


def _ada_kernel(ct_ref, w_ref, b_ref, o_ref, sb_ref, *, nb, tn):
    d = ct_ref.shape[0]

    @pl.when((pl.program_id(0) == 0) & (pl.program_id(1) == 0))
    def _():
        s = _silu(ct_ref[...])
        for b in range(nb):
            sb_ref[b] = jnp.broadcast_to(s[:, b:b + 1], (d, LANES))

    rows = []
    for b in range(nb):
        cols = []
        for t in range(tn // LANES):
            w = w_ref[0, :, t * LANES:(t + 1) * LANES]
            cols.append(jnp.sum(w * sb_ref[b], axis=0, keepdims=True))
        rows.append(jnp.concatenate(cols, axis=1))
    o_ref[0] = jnp.concatenate(rows, axis=0) + b_ref[0]


def _ada_modulation(c, ada_w, ada_b, tn=512):
    depth, d, n3 = ada_w.shape
    nb = c.shape[0]
    return pl.pallas_call(
        functools.partial(_ada_kernel, nb=nb, tn=tn),
        grid=(depth, n3 // tn),
        in_specs=[pl.BlockSpec((d, nb), lambda l, j: (0, 0)),
                  pl.BlockSpec((1, d, tn), lambda l, j: (l, 0, j)),
                  pl.BlockSpec((1, 1, tn), lambda l, j: (l, 0, j))],
        out_specs=pl.BlockSpec((1, nb, tn), lambda l, j: (l, 0, j)),
        out_shape=jax.ShapeDtypeStruct((depth, nb, n3), F32),
        scratch_shapes=[pltpu.VMEM((nb, d, LANES), F32)],
        compiler_params=_params(2),
        name="ada_modulation",
    )(c.T, ada_w, ada_b.reshape(depth, 1, n3))


def _rms(v, g):
    var = jnp.mean(v * v, axis=-1, keepdims=True)
    return v * lax.rsqrt(var + EPS) * g


def _resnorm_kernel(*refs, has_post, has_pre, has_kv):
    it = iter(refs)
    x_ref = next(it)
    if has_post:
        y_ref, gate_ref, gpost_ref = next(it), next(it), next(it)
    if has_pre:
        shift_ref, scale_ref, gpre_ref = next(it), next(it), next(it)
    if has_kv:
        gkv_ref = next(it)
    x = x_ref[0]
    if has_post:
        x = x + gate_ref[0] * _rms(y_ref[0].astype(F32), gpost_ref[...])
        next(it)[0] = x
    if has_pre:
        h = _rms(x, gpre_ref[...]) * (1.0 + scale_ref[0]) + shift_ref[0]
        next(it)[0] = h.astype(BF16)
    if has_kv:
        next(it)[0] = _rms(x, gkv_ref[...]).astype(BF16)


def _resnorm(x, post=None, pre=None, kv_g=None, tr=256):
    bsz, s, d = x.shape
    row = pl.BlockSpec((1, tr, d), lambda b, i: (b, i, 0))
    per_b = pl.BlockSpec((1, 1, d), lambda b, i: (b, 0, 0))
    vec = pl.BlockSpec((1, d), lambda b, i: (0, 0))
    args, in_specs, out_specs, out_shape = [x], [row], [], []
    if post is not None:
        y, gate, g_post = post
        args += [y, gate, g_post.reshape(1, d)]
        in_specs += [row, per_b, vec]
        out_specs.append(row)
        out_shape.append(jax.ShapeDtypeStruct((bsz, s, d), F32))
    if pre is not None:
        shift, scale, g_pre = pre
        args += [shift, scale, g_pre.reshape(1, d)]
        in_specs += [per_b, per_b, vec]
        out_specs.append(row)
        out_shape.append(jax.ShapeDtypeStruct((bsz, s, d), BF16))
    if kv_g is not None:
        args.append(kv_g.reshape(1, d))
        in_specs.append(vec)
        out_specs.append(row)
        out_shape.append(jax.ShapeDtypeStruct((bsz, s, d), BF16))
    return pl.pallas_call(
        functools.partial(_resnorm_kernel, has_post=post is not None, has_pre=pre is not None,
                          has_kv=kv_g is not None),
        grid=(bsz, s // tr), in_specs=in_specs, out_specs=out_specs, out_shape=out_shape,
        compiler_params=_params(2), name="resnorm",
    )(*args)


def _mm_kernel(a_ref, b_ref, o_ref):
    o_ref[...] = jnp.dot(a_ref[...], b_ref[...], preferred_element_type=F32).astype(o_ref.dtype)


def _matmul(a, b, out_dtype, tm, tn, name, col0=0, ncols=None):
    m, k = a.shape
    n = b.shape[1] - col0 if ncols is None else ncols
    tm, tn = min(tm, m), min(tn, n)
    jb0 = col0 // tn
    return pl.pallas_call(
        _mm_kernel, grid=(m // tm, n // tn),
        in_specs=[pl.BlockSpec((tm, k), lambda i, j: (i, 0)),
                  pl.BlockSpec((k, tn), lambda i, j: (0, jb0 + j))],
        out_specs=pl.BlockSpec((tm, tn), lambda i, j: (i, j)),
        out_shape=jax.ShapeDtypeStruct((m, n), out_dtype),
        compiler_params=_params(2), name=name,
    )(a, b)


CONV_HALO = 16


def _conv_kernel(u_ref, prev_ref, w_ref, b_ref, o_ref):
    ts = u_ref.shape[1]
    cur = u_ref[0].astype(F32)
    tail = prev_ref[0].astype(F32)
    tail = jnp.where(pl.program_id(1) == 0, 0.0, tail)
    ext = jnp.concatenate([tail, cur], axis=0)
    w = w_ref[...]
    acc = b_ref[...] + w[SSM_CONV - 1:SSM_CONV] * cur
    for k in range(SSM_CONV - 1):
        off = CONV_HALO - (SSM_CONV - 1) + k
        acc = acc + w[k:k + 1] * ext[off:off + ts]
    o_ref[0] = _silu(acc).astype(o_ref.dtype)


def _conv_silu(zx, conv_w, conv_b, col0, ts=512, tc=1024):
    bsz, s, _ = zx.shape
    c = conv_w.shape[1]
    cb0 = col0 // tc
    hb = ts // CONV_HALO
    return pl.pallas_call(
        _conv_kernel, grid=(bsz, s // ts, c // tc),
        in_specs=[pl.BlockSpec((1, ts, tc), lambda b, i, j: (b, i, cb0 + j)),
                  pl.BlockSpec((1, CONV_HALO, tc), lambda b, i, j: (b, jnp.maximum(i * hb - 1, 0), cb0 + j)),
                  pl.BlockSpec((SSM_CONV, tc), lambda b, i, j: (0, j)),
                  pl.BlockSpec((1, tc), lambda b, i, j: (0, j))],
        out_specs=pl.BlockSpec((1, ts, tc), lambda b, i, j: (b, i, j)),
        out_shape=jax.ShapeDtypeStruct((bsz, s, c), BF16),
        compiler_params=_params(3), name="conv_silu",
    )(zx, zx, conv_w, conv_b.reshape(1, c))


def _scan_kernel(x_ref, b_ref, c_ref, z_ref, dta_ref, dtt_ref, alr_ref, alc_ref, dbr_ref, dbc_ref,
                 dexp_ref, ng_ref, ex_ref, o_ref, st_ref, xbd_ref, *, nsub):
    L = CHUNK
    P = SSM_HEADDIM
    hp = LANES // P
    nheads = alr_ref.shape[2]
    npair = nheads // hp
    gw = nheads * P
    nquad = xbd_ref.shape[1]
    qh = gw // nquad // P

    @pl.when(pl.program_id(2) == 0)
    def _():
        st_ref[...] = jnp.zeros_like(st_ref)
        xbd_ref[...] = jnp.zeros_like(xbd_ref)

    a_r = -jnp.exp(alr_ref[0])
    a_c = -jnp.exp(alc_ref[0])
    lane = lax.broadcasted_iota(jnp.int32, (L, LANES), 1)
    sub = lax.broadcasted_iota(jnp.int32, (L, LANES), 0)
    lo = lane < P
    s_of_lane = lane & (P - 1)
    causal2 = s_of_lane <= sub
    tri = (lax.broadcasted_iota(jnp.int32, (L, L), 1) <= lax.broadcasted_iota(jnp.int32, (L, L), 0)).astype(BF16)
    tri_t2 = (sub <= s_of_lane).astype(BF16)
    blk = lax.broadcasted_iota(jnp.int32, (L, qh * P), 1) // P
    dexp = dexp_ref[...]
    ng = ng_ref[...]

    def pieces(v, n=3):
        out, rem = [], v
        for _ in range(n):
            hi = rem.astype(BF16).astype(F32)
            out.append(hi)
            rem = rem - hi
        return out

    def local(ci):
        rows = slice(ci * L, (ci + 1) * L)
        x = x_ref[0, rows, :]
        bm = b_ref[0, rows, :]
        cm = c_ref[0, rows, :]
        dt = _softplus(dta_ref[0, ci, 0] + dbr_ref[0])
        dt_t = _softplus(dtt_ref[0, ci, 0] + dbc_ref[0])
        acum = sum(jnp.dot(tri, p.astype(BF16), preferred_element_type=F32) for p in pieces(dt * a_r))
        acum_t = sum(jnp.dot(p.astype(BF16), tri_t2, preferred_element_type=F32)
                     for p in pieces(dt_t[:, :L] * a_c))
        e_in = jnp.exp(acum)
        e_out = jnp.exp(acum[L - 1:L, :] - acum) * dt
        ex = ex_ref[...]
        spr = lambda v, n: jnp.dot(jnp.concatenate(pieces(v, n), axis=1).astype(BF16), ex[:n * nheads],
                                   preferred_element_type=F32)
        col, sc_in, sc_out = spr(acum, 3), spr(e_in, 2), spr(e_out, 2)
        cb2 = lax.dot_general(cm, jnp.concatenate([bm, bm], axis=0), (((1,), (1,)), ((), ())),
                              preferred_element_type=F32)
        xf = x.astype(F32)

        w_tiles = []
        for j in range(npair):
            r0, r1 = hp * j, hp * j + 1
            row = jnp.where(lo[0:1], acum_t[r0:r0 + 1], acum_t[r1:r1 + 1])
            dtr = jnp.where(lo[0:1], dt_t[r0:r0 + 1], dt_t[r1:r1 + 1])
            seg = jnp.where(causal2, col[:, j * LANES:(j + 1) * LANES] - row, NEG)
            w_tiles.append((jnp.exp(seg) * (cb2 * dtr)).astype(BF16))

        y_parts = []
        for q in range(nquad):
            w4 = jnp.concatenate(w_tiles[q * qh // hp:(q + 1) * qh // hp], axis=1)
            x4 = x[:, q * qh * P:(q + 1) * qh * P]
            xbd = jnp.concatenate([jnp.where(blk == i, x4, jnp.zeros_like(x4)) for i in range(qh)], axis=0)
            y_parts.append(jnp.dot(w4, xbd, preferred_element_type=F32))
        y_loc = jnp.concatenate(y_parts, axis=1) + xf * dexp

        xw = x * sc_out.astype(BF16)
        upd = lax.dot_general(bm, xw, (((0,), (0,)), ((), ())), preferred_element_type=F32)
        return cm, sc_in, y_loc, upd

    def carried(ci, st, cm, sc_in, y_loc, upd):
        rows = slice(ci * L, (ci + 1) * L)
        y = y_loc + jnp.dot(cm, st.astype(BF16), preferred_element_type=F32) * sc_in
        yg = y * _silu(z_ref[0, rows, :].astype(F32))
        o_ref[0, rows, :] = _rms(yg, ng).astype(o_ref.dtype)
        return st * sc_in[L - 1:L, :] + upd

    st = st_ref[...]
    for ci in range(nsub):
        st = carried(ci, st, *local(ci))
    st_ref[...] = st


def _spread_matrix(nheads_g, parts=3, nq=3):
    rows = np.arange(parts * nq * nheads_g)
    cols = np.arange(nq * nheads_g * SSM_HEADDIM)
    q_r = rows % (nq * nheads_g)
    return jnp.asarray(q_r[:, None] == (cols // SSM_HEADDIM)[None, :], dtype=BF16)


def _ssd_scan(xbc, zx, dt_raw, dt_bias, a_log, d_skip, norm_g, nsub=8):
    bsz, s, _ = xbc.shape
    g = SSM_GROUPS
    nheads = dt_raw.shape[-1]
    r = nheads // g
    gw = r * SSM_HEADDIM
    d_inner = nheads * SSM_HEADDIM
    nc = s // CHUNK
    lb = nsub * CHUNK
    dt5 = dt_raw.reshape(bsz, nc, CHUNK, g, r)
    dt_a = dt5.transpose(0, 1, 3, 2, 4)
    dt_t = dt5.transpose(0, 1, 3, 4, 2)
    dt_t = jnp.concatenate([dt_t, dt_t], axis=-1)
    nb_off = d_inner // SSM_STATE
    spread = _spread_matrix(r, nq=1)
    per_g_row = lambda arr: arr.reshape(g, 1, r)
    per_g_col = lambda arr: arr.reshape(g, r, 1)
    row_spec = pl.BlockSpec((1, 1, r), lambda b, gi, ci: (gi, 0, 0))
    col_spec = pl.BlockSpec((1, r, 1), lambda b, gi, ci: (gi, 0, 0))
    wide = pl.BlockSpec((1, lb, gw), lambda b, gi, ci: (b, ci, gi))
    return pl.pallas_call(
        functools.partial(_scan_kernel, nsub=nsub),
        grid=(bsz, g, nc // nsub),
        in_specs=[wide,
                  pl.BlockSpec((1, lb, SSM_STATE), lambda b, gi, ci: (b, ci, nb_off + gi)),
                  pl.BlockSpec((1, lb, SSM_STATE), lambda b, gi, ci: (b, ci, nb_off + g + gi)),
                  wide,
                  pl.BlockSpec((1, nsub, 1, CHUNK, r), lambda b, gi, ci: (b, ci, gi, 0, 0)),
                  pl.BlockSpec((1, nsub, 1, r, 2 * CHUNK), lambda b, gi, ci: (b, ci, gi, 0, 0)),
                  row_spec, col_spec, row_spec, col_spec,
                  pl.BlockSpec((1, gw), lambda b, gi, ci: (0, gi)),
                  pl.BlockSpec((1, gw), lambda b, gi, ci: (0, gi)),
                  pl.BlockSpec(spread.shape, lambda b, gi, ci: (0, 0))],
        out_specs=wide,
        out_shape=jax.ShapeDtypeStruct((bsz, s, d_inner), BF16),
        scratch_shapes=[pltpu.VMEM((SSM_STATE, gw), F32),
                        pltpu.VMEM((nsub, gw // MXU_DIM, MXU_DIM, MXU_DIM), BF16)],
        compiler_params=_params(3), name="ssd_scan",
    )(xbc, xbc, xbc, zx, dt_a, dt_t, per_g_row(a_log), per_g_col(a_log), per_g_row(dt_bias),
      per_g_col(dt_bias), jnp.repeat(d_skip, SSM_HEADDIM).reshape(1, d_inner), norm_g.reshape(1, d_inner), spread)


def _attn_kernel(q_ref, z_ref, k0_ref, k1_ref, k2_ref, v0_ref, v1_ref, v2_ref, rv_ref, o_ref, bias_sc, *, hb):
    i = pl.program_id(2)

    @pl.when((pl.program_id(1) == 0) & (i == 0))
    def _():
        l = lax.broadcasted_iota(jnp.int32, (ATT_QBLK, ATT_KBLK), 0)
        j = lax.broadcasted_iota(jnp.int32, (ATT_QBLK, ATT_KBLK), 1)
        lc, jc = l // CHUNK, j // CHUNK
        band = (jc >= lc) & (jc <= lc + ATT_LEFT_CHUNKS)
        for h in range(hb):
            src = jnp.broadcast_to(rv_ref[h], (ATT_QBLK, ATT_RVEC))
            toe = pltpu.roll(src, ATT_RVEC - ATT_QBLK, 1, stride=1, stride_axis=0)[:, :ATT_KBLK]
            for v in range(ATT_VARIANTS):
                ok = band & (j >= (ATT_KBLK - ATT_QBLK) - ATT_QBLK * v)
                b = jnp.where(ok, toe, NEG)
                for half in range(2):
                    bias_sc[v, h, half] = b[half * ATT_HALF:(half + 1) * ATT_HALF,
                                            half * ATT_HALF:half * ATT_HALF + ATT_WIN]

    v = jnp.minimum(i, ATT_VARIANTS - 1)
    units = [(h, half) for h in range(hb) for half in range(2)]

    def lanes(h):
        return slice(h * ATT_HEADDIM, (h + 1) * ATT_HEADDIM)

    def rows(half):
        return slice(half * ATT_HALF, (half + 1) * ATT_HALF)

    def window(refs, h, half):
        cat = jnp.concatenate([r[0, :, lanes(h)] for r in refs], axis=0)
        return cat[half * ATT_HALF:half * ATT_HALF + ATT_WIN]

    def scores(h, half):
        s = lax.dot_general(q_ref[0, rows(half), lanes(h)], window((k0_ref, k1_ref, k2_ref), h, half),
                            (((1,), (1,)), ((), ())), preferred_element_type=F32)
        return s + bias_sc[v, h, half]

    def probs(s):
        p = jnp.exp2(s - jnp.max(s, axis=1, keepdims=True))
        return p.astype(BF16), jnp.sum(p, axis=1, keepdims=True)

    def finish(h, half, p, den):
        o = jnp.dot(p, window((v0_ref, v1_ref, v2_ref), h, half), preferred_element_type=F32) / den
        o_ref[0, rows(half), lanes(h)] = (o * _silu(z_ref[0, rows(half), lanes(h)].astype(F32))).astype(o_ref.dtype)

    s_q, p_q = {}, {}
    n = len(units)
    for step in range(n + 2):
        if step < n:
            s_q[step] = scores(*units[step])
        if 1 <= step <= n:
            p_q[step - 1] = probs(s_q.pop(step - 1))
        if step >= 2:
            finish(*units[step - 2], *p_q.pop(step - 2))


def _rel_source_rows(rel_table):
    u = np.arange(ATT_RVEC)
    bucket = np.clip(ATT_KBLK - u, -REL_MAX, REL_MAX) + REL_MAX
    return (rel_table[:, bucket].astype(F32) * LOG2E)[:, None, :]


def _attention(qz, kv, rel_rows, hb=4):
    bsz, s, two_d = qz.shape
    d = two_d // 2
    hw = hb * ATT_HEADDIM
    ng = d // hw
    cur = lambda off: pl.BlockSpec((1, ATT_QBLK, hw), lambda hg, b, i: (b, i, off + hg))
    back = lambda off, n: pl.BlockSpec((1, ATT_QBLK, hw), lambda hg, b, i: (b, jnp.maximum(i - n, 0), off + hg))
    return pl.pallas_call(
        functools.partial(_attn_kernel, hb=hb),
        grid=(ng, bsz, s // ATT_QBLK),
        in_specs=[cur(0), cur(ng), back(0, 2), back(0, 1), cur(0), back(ng, 2), back(ng, 1), cur(ng),
                  pl.BlockSpec((hb, 1, ATT_RVEC), lambda hg, b, i: (hg, 0, 0))],
        out_specs=pl.BlockSpec((1, ATT_QBLK, hw), lambda hg, b, i: (b, i, hg)),
        out_shape=jax.ShapeDtypeStruct((bsz, s, d), BF16),
        scratch_shapes=[pltpu.VMEM((ATT_VARIANTS, hb, 2, ATT_HALF, ATT_WIN), F32)],
        compiler_params=_params(3), name="band_attention",
    )(qz, qz, kv, kv, kv, kv, kv, kv, rel_rows)


def kernel(x, c, ada_w, ada_b, pre_norm_g, post_norm_g, ssm_w_in, ssm_conv_w, ssm_conv_b, ssm_dt_bias, ssm_a_log, ssm_d, ssm_norm_g, ssm_w_out, kv_norm_g, w_kv, att_w_in, att_rel_bias, att_w_out):
    bsz, s, d = x.shape
    depth = ada_w.shape[0]
    n_a = ssm_w_in.shape[0]
    t = bsz * s
    d_inner = ssm_w_out.shape[1]
    nheads = ssm_dt_bias.shape[1]
    zx_cols = ssm_w_in.shape[2] - nheads

    mod = _ada_modulation(c, ada_w, ada_b)
    shift = [mod[l, :, None, 0:d] for l in range(depth)]
    scale = [mod[l, :, None, d:2 * d] for l in range(depth)]
    gate = [mod[l, :, None, 2 * d:3 * d] for l in range(depth)]

    (h,) = _resnorm(x, pre=(shift[0], scale[0], pre_norm_g[0]))
    kv = None
    for layer in range(depth):
        h2 = h.reshape(t, d)
        if layer < n_a:
            i = layer
            w_in = ssm_w_in[i].astype(BF16)
            zx = _matmul(h2, w_in, BF16, 1024, 1024, "ssm_in_proj", ncols=zx_cols)
            dt_raw = _matmul(h2, w_in, F32, 1024, LANES, "ssm_dt_proj", col0=zx_cols, ncols=nheads)
            zx = zx.reshape(bsz, s, zx_cols)
            xbc = _conv_silu(zx, ssm_conv_w[i], ssm_conv_b[i], col0=d_inner)
            yn = _ssd_scan(xbc, zx, dt_raw.reshape(bsz, s, nheads), ssm_dt_bias[i], ssm_a_log[i],
                           ssm_d[i], ssm_norm_g[i])
            y = _matmul(yn.reshape(t, d_inner), ssm_w_out[i].astype(BF16), BF16, 1024, 512, "ssm_out_proj")
        else:
            i = layer - n_a
            att_d = att_w_out.shape[1]
            qscale = jnp.concatenate([jnp.full((att_d,), ATT_HEADDIM ** -0.5 * LOG2E, F32), jnp.ones((att_d,), F32)])
            qz = _matmul(h2, (att_w_in[i] * qscale).astype(BF16), BF16, 1024, 1024, "att_in_proj")
            o = _attention(qz.reshape(bsz, s, 2 * att_d), kv, _rel_source_rows(att_rel_bias[i]))
            y = _matmul(o.reshape(t, att_d), att_w_out[i].astype(BF16), BF16, 1024, 1024, "att_out_proj")
        post = (y.reshape(bsz, s, d), gate[layer], post_norm_g[layer])
        if layer + 1 < depth:
            nxt = layer + 1
            pre = (shift[nxt], scale[nxt], pre_norm_g[nxt])
            if nxt == n_a:
                x, h, hkv = _resnorm(x, post=post, pre=pre, kv_g=kv_norm_g)
                kv = _matmul(hkv.reshape(t, d), w_kv.astype(BF16), BF16, 1024, 1024, "kv_proj")
                kv = kv.reshape(bsz, s, w_kv.shape[1])
            else:
                x, h = _resnorm(x, post=post, pre=pre)
        else:
            (x,) = _resnorm(x, post=post)
    return x
```

```python
import functools

import jax
import jax.numpy as jnp
import numpy as np
from jax import lax
from jax.experimental import pallas as pl
from jax.experimental.pallas import tpu as pltpu

F32 = jnp.float32
BF16 = jnp.bfloat16

EPS = 1e-6
CHUNK = 64
LANES = 128
MXU_DIM = 256
V7X_VMEM_BYTES = 64 * 1024 * 1024
VMEM_LIMIT = V7X_VMEM_BYTES - 8 * 1024 * 1024

SSM_HEADDIM = 64
SSM_GROUPS = 8
SSM_STATE = 128
SSM_CONV = 4
ATT_HEADDIM = 128
ATT_LEFT_CHUNKS = 8
REL_MAX = 128
ATT_QBLK = 4 * CHUNK
ATT_KBLK = 3 * ATT_QBLK
ATT_HALF = ATT_QBLK // 2
ATT_WIN = ATT_KBLK - ATT_HALF
ATT_RVEC = 1024
ATT_VARIANTS = 3
LOG2E = 1.4426950408889634
NEG = -1e30


def _params(n_grid, vmem=VMEM_LIMIT):
    return pltpu.CompilerParams(dimension_semantics=("arbitrary",) * n_grid, vmem_limit_bytes=vmem)


def _silu(v):
    return v * jax.nn.sigmoid(v)


def _softplus(v):
    return jnp.maximum(v, 0.0) + jnp.log1p(jnp.exp(-jnp.abs(v)))


def _ada_kernel(ct_ref, w_ref, b_ref, o_ref, sb_ref, *, nb, tn):
    d = ct_ref.shape[0]

    @pl.when((pl.program_id(0) == 0) & (pl.program_id(1) == 0))
    def _():
        s = _silu(ct_ref[...])
        for b in range(nb):
            sb_ref[b] = jnp.broadcast_to(s[:, b:b + 1], (d, LANES))

    rows = []
    for b in range(nb):
        cols = []
        for t in range(tn // LANES):
            w = w_ref[0, :, t * LANES:(t + 1) * LANES]
            cols.append(jnp.sum(w * sb_ref[b], axis=0, keepdims=True))
        rows.append(jnp.concatenate(cols, axis=1))
    o_ref[0] = jnp.concatenate(rows, axis=0) + b_ref[0]


def _ada_modulation(c, ada_w, ada_b, tn=512):
    depth, d, n3 = ada_w.shape
    nb = c.shape[0]
    return pl.pallas_call(
        functools.partial(_ada_kernel, nb=nb, tn=tn),
        grid=(depth, n3 // tn),
        in_specs=[pl.BlockSpec((d, nb), lambda l, j: (0, 0)),
                  pl.BlockSpec((1, d, tn), lambda l, j: (l, 0, j)),
                  pl.BlockSpec((1, 1, tn), lambda l, j: (l, 0, j))],
        out_specs=pl.BlockSpec((1, nb, tn), lambda l, j: (l, 0, j)),
        out_shape=jax.ShapeDtypeStruct((depth, nb, n3), F32),
        scratch_shapes=[pltpu.VMEM((nb, d, LANES), F32)],
        compiler_params=_params(2),
        name="ada_modulation",
    )(c.T, ada_w, ada_b.reshape(depth, 1, n3))


def _rms(v, g):
    var = jnp.mean(v * v, axis=-1, keepdims=True)
    return v * lax.rsqrt(var + EPS) * g


def _resnorm_kernel(*refs, has_post, has_pre, has_kv):
    it = iter(refs)
    x_ref = next(it)
    if has_post:
        y_ref, gate_ref, gpost_ref = next(it), next(it), next(it)
    if has_pre:
        shift_ref, scale_ref, gpre_ref = next(it), next(it), next(it)
    if has_kv:
        gkv_ref = next(it)
    x = x_ref[0]
    if has_post:
        x = x + gate_ref[0] * _rms(y_ref[0].astype(F32), gpost_ref[...])
        next(it)[0] = x
    if has_pre:
        h = _rms(x, gpre_ref[...]) * (1.0 + scale_ref[0]) + shift_ref[0]
        next(it)[0] = h.astype(BF16)
    if has_kv:
        next(it)[0] = _rms(x, gkv_ref[...]).astype(BF16)


def _resnorm(x, post=None, pre=None, kv_g=None, tr=256):
    bsz, s, d = x.shape
    row = pl.BlockSpec((1, tr, d), lambda b, i: (b, i, 0))
    per_b = pl.BlockSpec((1, 1, d), lambda b, i: (b, 0, 0))
    vec = pl.BlockSpec((1, d), lambda b, i: (0, 0))
    args, in_specs, out_specs, out_shape = [x], [row], [], []
    if post is not None:
        y, gate, g_post = post
        args += [y, gate, g_post.reshape(1, d)]
        in_specs += [row, per_b, vec]
        out_specs.append(row)
        out_shape.append(jax.ShapeDtypeStruct((bsz, s, d), F32))
    if pre is not None:
        shift, scale, g_pre = pre
        args += [shift, scale, g_pre.reshape(1, d)]
        in_specs += [per_b, per_b, vec]
        out_specs.append(row)
        out_shape.append(jax.ShapeDtypeStruct((bsz, s, d), BF16))
    if kv_g is not None:
        args.append(kv_g.reshape(1, d))
        in_specs.append(vec)
        out_specs.append(row)
        out_shape.append(jax.ShapeDtypeStruct((bsz, s, d), BF16))
    return pl.pallas_call(
        functools.partial(_resnorm_kernel, has_post=post is not None, has_pre=pre is not None,
                          has_kv=kv_g is not None),
        grid=(bsz, s // tr), in_specs=in_specs, out_specs=out_specs, out_shape=out_shape,
        compiler_params=_params(2), name="resnorm",
    )(*args)


def _mm_kernel(a_ref, b_ref, *rest, act, act_from):
    o_ref = rest[-1]
    tn = o_ref.shape[1]
    sub = min(MXU_DIM, tn)

    def body(fn):
        for c in range(tn // sub):
            cs = slice(c * sub, (c + 1) * sub)
            acc = jnp.dot(a_ref[...], b_ref[:, cs], preferred_element_type=F32)
            o_ref[:, cs] = fn(acc, cs).astype(o_ref.dtype)

    if act == "silu":
        pl.when(pl.program_id(1) < act_from)(lambda: body(lambda acc, cs: acc))
        pl.when(pl.program_id(1) >= act_from)(lambda: body(lambda acc, cs: _silu(acc)))
    elif act == "softplus":
        body(lambda acc, cs: _softplus(acc + rest[0][:, cs]))
    else:
        body(lambda acc, cs: acc)


def _matmul(a, b, layer, out_dtype, tm, tn, name, col0=0, ncols=None, act=None, act_from=0, bias=None):
    m, k = a.shape
    n = b.shape[2] - col0 if ncols is None else ncols
    tm, tn = min(tm, m), min(tn, n)
    jb0 = col0 // tn
    args = [a, b]
    in_specs = [pl.BlockSpec((tm, k), lambda i, j: (i, 0)),
                pl.BlockSpec((None, k, tn), lambda i, j: (layer, 0, jb0 + j))]
    if bias is not None:
        args.append(bias.reshape(1, n))
        in_specs.append(pl.BlockSpec((1, tn), lambda i, j: (0, j)))
    return pl.pallas_call(
        functools.partial(_mm_kernel, act=act, act_from=act_from), grid=(m // tm, n // tn),
        in_specs=in_specs,
        out_specs=pl.BlockSpec((tm, tn), lambda i, j: (i, j)),
        out_shape=jax.ShapeDtypeStruct((m, n), out_dtype),
        compiler_params=_params(2), name=name,
    )(*args)


CONV_HALO = 8


def _ssm_in_kernel(a_ref, w_ref, cw_ref, cb_ref, o_ref, stage_ref, tail_ref, *, n_plain, tiles_per_seq):
    i, j = pl.program_id(0), pl.program_id(1)
    tm, tn = o_ref.shape
    sub = min(MXU_DIM, tn)

    @pl.when((i == 0) & (j == 0))
    def _():
        tail_ref[...] = jnp.zeros_like(tail_ref)

    def cols(c):
        return slice(c * sub, (c + 1) * sub)

    def pipelined(epilogue):
        n_sub = tn // sub
        acc = jnp.dot(a_ref[...], w_ref[:, cols(0)], preferred_element_type=F32)
        for c in range(n_sub):
            nxt = jnp.dot(a_ref[...], w_ref[:, cols(c + 1)], preferred_element_type=F32) if c + 1 < n_sub else None
            epilogue(acc, cols(c))
            acc = nxt

    @pl.when(j < n_plain)
    def _():
        def gate(acc, cs):
            o_ref[:, cs] = _silu(acc).astype(o_ref.dtype)
        pipelined(gate)

    @pl.when(j >= n_plain)
    def _():
        jc = j - n_plain
        seq_start = i % tiles_per_seq == 0

        def conv(acc, cs):
            stage_ref[0:CONV_HALO, cs] = jnp.where(seq_start, 0.0, tail_ref[jc, :, cs])
            stage_ref[CONV_HALO:, cs] = acc
            tail_ref[jc, :, cs] = acc[tm - CONV_HALO:]
            w = cw_ref[:, cs]
            out = cb_ref[:, cs] + w[SSM_CONV - 1:SSM_CONV] * acc
            for k in range(SSM_CONV - 1):
                off = CONV_HALO - (SSM_CONV - 1) + k
                out = out + w[k:k + 1] * stage_ref[off:off + tm, cs]
            o_ref[:, cs] = _silu(out).astype(o_ref.dtype)
        pipelined(conv)


def _ssm_in_proj(h2, w_in, layer, conv_w, conv_b, n_plain_cols, seq, tm=1024, tn=1024):
    t, k = h2.shape
    c = conv_w.shape[1]
    n = n_plain_cols + c
    n_plain = n_plain_cols // tn
    conv_col = lambda i, j: (0, jnp.maximum(j - n_plain, 0))
    return pl.pallas_call(
        functools.partial(_ssm_in_kernel, n_plain=n_plain, tiles_per_seq=seq // tm),
        grid=(t // tm, n // tn),
        in_specs=[pl.BlockSpec((tm, k), lambda i, j: (i, 0)),
                  pl.BlockSpec((None, k, tn), lambda i, j: (layer, 0, j)),
                  pl.BlockSpec((SSM_CONV, tn), conv_col),
                  pl.BlockSpec((1, tn), conv_col)],
        out_specs=pl.BlockSpec((tm, tn), lambda i, j: (i, j)),
        out_shape=jax.ShapeDtypeStruct((t, n), BF16),
        scratch_shapes=[pltpu.VMEM((CONV_HALO + tm, tn), F32),
                        pltpu.VMEM((c // tn, CONV_HALO, tn), F32)],
        compiler_params=_params(2), name="ssm_in_proj",
    )(h2, w_in, conv_w, conv_b.reshape(1, c))


def _scan_kernel(x_ref, b_ref, c_ref, z_ref, dta_ref, dtt_ref, alr_ref, alc_ref,
                 dexp_ref, ng_ref, ex_ref, o_ref, st_ref, *, nsub):
    L = CHUNK
    P = SSM_HEADDIM
    hp = LANES // P
    nheads = alr_ref.shape[2]
    npair = nheads // hp
    qh = MXU_DIM // P
    nquad = nheads // qh

    @pl.when(pl.program_id(2) == 0)
    def _():
        st_ref[...] = jnp.zeros_like(st_ref)

    a_r = -jnp.exp(alr_ref[0])
    a_c = -jnp.exp(alc_ref[0])
    lane = lax.broadcasted_iota(jnp.int32, (L, LANES), 1)
    sub = lax.broadcasted_iota(jnp.int32, (L, LANES), 0)
    lo = lane < P
    s_of_lane = lane & (P - 1)
    causal2 = s_of_lane <= sub
    tri = (lax.broadcasted_iota(jnp.int32, (L, L), 1) <= lax.broadcasted_iota(jnp.int32, (L, L), 0)).astype(BF16)
    tri_t2 = (sub <= s_of_lane).astype(BF16)
    blk = lax.broadcasted_iota(jnp.int32, (L, qh * P), 1) // P
    dexp = dexp_ref[...]
    ng = ng_ref[...]

    def pieces(v, n=3):
        out, rem = [], v
        for _ in range(n):
            hi = rem.astype(BF16).astype(F32)
            out.append(hi)
            rem = rem - hi
        return out

    def local(ci):
        rows = slice(ci * L, (ci + 1) * L)
        x = x_ref[0, rows, :]
        bm = b_ref[0, rows, :]
        cm = c_ref[0, rows, :]
        dt = dta_ref[0, ci, 0]
        dt_t = dtt_ref[0, ci, 0]
        acum = sum(jnp.dot(tri, p.astype(BF16), preferred_element_type=F32) for p in pieces(dt * a_r))
        acum_t = sum(jnp.dot(p.astype(BF16), tri_t2, preferred_element_type=F32)
                     for p in pieces(dt_t[:, :L] * a_c))
        e_in = jnp.exp(acum)
        e_out = jnp.exp(acum[L - 1:L, :] - acum) * dt
        ex = ex_ref[...]
        spr = lambda v, n: jnp.dot(jnp.concatenate(pieces(v, n), axis=1).astype(BF16), ex[:n * nheads],
                                   preferred_element_type=F32)
        col, sc_in, sc_out = spr(acum, 3), spr(e_in, 2), spr(e_out, 2)
        cb2 = lax.dot_general(cm, jnp.concatenate([bm, bm], axis=0), (((1,), (1,)), ((), ())),
                              preferred_element_type=F32)
        xf = x.astype(F32)

        w_tiles = []
        for j in range(npair):
            r0, r1 = hp * j, hp * j + 1
            row = jnp.where(lo[0:1], acum_t[r0:r0 + 1], acum_t[r1:r1 + 1])
            dtr = jnp.where(lo[0:1], dt_t[r0:r0 + 1], dt_t[r1:r1 + 1])
            seg = jnp.where(causal2, col[:, j * LANES:(j + 1) * LANES] - row, NEG)
            w_tiles.append((jnp.exp(seg) * (cb2 * dtr)).astype(BF16))

        y_parts = []
        for q in range(nquad):
            w4 = jnp.concatenate(w_tiles[q * qh // hp:(q + 1) * qh // hp], axis=1)
            x4 = x[:, q * qh * P:(q + 1) * qh * P]
            xbd = jnp.concatenate([jnp.where(blk == i, x4, jnp.zeros_like(x4)) for i in range(qh)], axis=0)
            y_parts.append(jnp.dot(w4, xbd, preferred_element_type=F32))
        y_loc = jnp.concatenate(y_parts, axis=1) + xf * dexp

        xw = x * sc_out.astype(BF16)
        upd = lax.dot_general(bm, xw, (((0,), (0,)), ((), ())), preferred_element_type=F32)
        return cm, sc_in, y_loc, upd

    def carried(ci, st, cm, sc_in, y_loc, upd):
        rows = slice(ci * L, (ci + 1) * L)
        y = y_loc + jnp.dot(cm, st.astype(BF16), preferred_element_type=F32) * sc_in
        yg = y * z_ref[0, rows, :].astype(F32)
        o_ref[0, rows, :] = _rms(yg, ng).astype(o_ref.dtype)
        return st * sc_in[L - 1:L, :] + upd

    st = st_ref[...]
    for ci in range(nsub):
        st = carried(ci, st, *local(ci))
    st_ref[...] = st


def _spread_matrix(nheads_g, parts=3):
    rows = np.arange(parts * nheads_g)
    cols = np.arange(nheads_g * SSM_HEADDIM)
    return jnp.asarray((rows % nheads_g)[:, None] == (cols // SSM_HEADDIM)[None, :], dtype=BF16)


def _ssd_scan(zx, dt, a_log, d_skip, norm_g, nsub=8):
    bsz, s, _ = zx.shape
    g = SSM_GROUPS
    nheads = dt.shape[-1]
    r = nheads // g
    gw = r * SSM_HEADDIM
    d_inner = nheads * SSM_HEADDIM
    nc = s // CHUNK
    lb = nsub * CHUNK
    dt5 = dt.reshape(bsz, nc, CHUNK, g, r)
    dt_a = dt5.transpose(0, 1, 3, 2, 4)
    dt_t = dt5.transpose(0, 1, 3, 4, 2)
    dt_t = jnp.concatenate([dt_t, dt_t], axis=-1)
    x_off = d_inner // gw
    b_off = 2 * d_inner // SSM_STATE
    spread = _spread_matrix(r)
    row_spec = pl.BlockSpec((1, 1, r), lambda b, gi, ci: (gi, 0, 0))
    col_spec = pl.BlockSpec((1, r, 1), lambda b, gi, ci: (gi, 0, 0))
    wide = lambda off: pl.BlockSpec((1, lb, gw), lambda b, gi, ci: (b, ci, off + gi))
    narrow = lambda off: pl.BlockSpec((1, lb, SSM_STATE), lambda b, gi, ci: (b, ci, off + gi))
    return pl.pallas_call(
        functools.partial(_scan_kernel, nsub=nsub),
        grid=(bsz, g, nc // nsub),
        in_specs=[wide(x_off), narrow(b_off), narrow(b_off + g), wide(0),
                  pl.BlockSpec((1, nsub, 1, CHUNK, r), lambda b, gi, ci: (b, ci, gi, 0, 0)),
                  pl.BlockSpec((1, nsub, 1, r, 2 * CHUNK), lambda b, gi, ci: (b, ci, gi, 0, 0)),
                  row_spec, col_spec,
                  pl.BlockSpec((1, gw), lambda b, gi, ci: (0, gi)),
                  pl.BlockSpec((1, gw), lambda b, gi, ci: (0, gi)),
                  pl.BlockSpec(spread.shape, lambda b, gi, ci: (0, 0))],
        out_specs=wide(0),
        out_shape=jax.ShapeDtypeStruct((bsz, s, d_inner), BF16),
        scratch_shapes=[pltpu.VMEM((SSM_STATE, gw), F32)],
        compiler_params=_params(3), name="ssd_scan",
    )(zx, zx, zx, zx, dt_a, dt_t, a_log.reshape(g, 1, r), a_log.reshape(g, r, 1),
      jnp.repeat(d_skip, SSM_HEADDIM).reshape(1, d_inner), norm_g.reshape(1, d_inner), spread)


def _attn_kernel(q_ref, z_ref, k0_ref, k1_ref, k2_ref, v0_ref, v1_ref, v2_ref, rv_ref, o_ref, bias_sc, *, hb):
    i = pl.program_id(2)

    @pl.when((pl.program_id(1) == 0) & (i == 0))
    def _():
        l = lax.broadcasted_iota(jnp.int32, (ATT_QBLK, ATT_KBLK), 0)
        j = lax.broadcasted_iota(jnp.int32, (ATT_QBLK, ATT_KBLK), 1)
        lc, jc = l // CHUNK, j // CHUNK
        band = (jc >= lc) & (jc <= lc + ATT_LEFT_CHUNKS)
        for h in range(hb):
            src = jnp.broadcast_to(rv_ref[h], (ATT_QBLK, ATT_RVEC))
            toe = pltpu.roll(src, ATT_RVEC - ATT_QBLK, 1, stride=1, stride_axis=0)[:, :ATT_KBLK]
            for v in range(ATT_VARIANTS):
                ok = band & (j >= (ATT_KBLK - ATT_QBLK) - ATT_QBLK * v)
                b = jnp.where(ok, toe, NEG)
                for half in range(2):
                    bias_sc[v, h, half] = b[half * ATT_HALF:(half + 1) * ATT_HALF,
                                            half * ATT_HALF:half * ATT_HALF + ATT_WIN]

    v = jnp.minimum(i, ATT_VARIANTS - 1)
    units = [(h, half) for h in range(hb) for half in range(2)]

    def lanes(h):
        return slice(h * ATT_HEADDIM, (h + 1) * ATT_HEADDIM)

    def rows(half):
        return slice(half * ATT_HALF, (half + 1) * ATT_HALF)

    def window(refs, h, half):
        cat = jnp.concatenate([r[0, :, lanes(h)] for r in refs], axis=0)
        return cat[half * ATT_HALF:half * ATT_HALF + ATT_WIN]

    def scores(h, half):
        s = lax.dot_general(q_ref[0, rows(half), lanes(h)], window((k0_ref, k1_ref, k2_ref), h, half),
                            (((1,), (1,)), ((), ())), preferred_element_type=F32)
        return s + bias_sc[v, h, half]

    def probs(s):
        p = jnp.exp2(s - jnp.max(s, axis=1, keepdims=True))
        return p.astype(BF16), jnp.sum(p, axis=1, keepdims=True)

    def finish(h, half, p, den):
        o = jnp.dot(p, window((v0_ref, v1_ref, v2_ref), h, half), preferred_element_type=F32) / den
        o_ref[0, rows(half), lanes(h)] = (o * z_ref[0, rows(half), lanes(h)].astype(F32)).astype(o_ref.dtype)

    s_q, p_q = {}, {}
    n = len(units)
    for step in range(n + 2):
        if step < n:
            s_q[step] = scores(*units[step])
        if 1 <= step <= n:
            p_q[step - 1] = probs(s_q.pop(step - 1))
        if step >= 2:
            finish(*units[step - 2], *p_q.pop(step - 2))


def _rel_source_rows(rel_table):
    u = np.arange(ATT_RVEC)
    bucket = np.clip(ATT_KBLK - u, -REL_MAX, REL_MAX) + REL_MAX
    return (rel_table[:, bucket].astype(F32) * LOG2E)[:, None, :]


def _attention(qz, kv, rel_rows, hb=8):
    bsz, s, two_d = qz.shape
    d = two_d // 2
    hw = hb * ATT_HEADDIM
    ng = d // hw
    cur = lambda off: pl.BlockSpec((1, ATT_QBLK, hw), lambda hg, b, i: (b, i, off + hg))
    back = lambda off, n: pl.BlockSpec((1, ATT_QBLK, hw), lambda hg, b, i: (b, jnp.maximum(i - n, 0), off + hg))
    return pl.pallas_call(
        functools.partial(_attn_kernel, hb=hb),
        grid=(ng, bsz, s // ATT_QBLK),
        in_specs=[cur(0), cur(ng), back(0, 2), back(0, 1), cur(0), back(ng, 2), back(ng, 1), cur(ng),
                  pl.BlockSpec((hb, 1, ATT_RVEC), lambda hg, b, i: (hg, 0, 0))],
        out_specs=pl.BlockSpec((1, ATT_QBLK, hw), lambda hg, b, i: (b, i, hg)),
        out_shape=jax.ShapeDtypeStruct((bsz, s, d), BF16),
        scratch_shapes=[pltpu.VMEM((ATT_VARIANTS, hb, 2, ATT_HALF, ATT_WIN), F32)],
        compiler_params=_params(3), name="band_attention",
    )(qz, qz, kv, kv, kv, kv, kv, kv, rel_rows)


def kernel(x, c, ada_w, ada_b, pre_norm_g, post_norm_g, ssm_w_in, ssm_conv_w, ssm_conv_b, ssm_dt_bias, ssm_a_log, ssm_d, ssm_norm_g, ssm_w_out, kv_norm_g, w_kv, att_w_in, att_rel_bias, att_w_out):
    bsz, s, d = x.shape
    depth = ada_w.shape[0]
    n_a = ssm_w_in.shape[0]
    t = bsz * s
    d_inner = ssm_w_out.shape[1]
    nheads = ssm_dt_bias.shape[1]
    zx_cols = ssm_w_in.shape[2] - nheads

    mod = _ada_modulation(c, ada_w, ada_b)
    shift = [mod[l, :, None, 0:d] for l in range(depth)]
    scale = [mod[l, :, None, d:2 * d] for l in range(depth)]
    gate = [mod[l, :, None, 2 * d:3 * d] for l in range(depth)]

    ssm_w_in_b = ssm_w_in.astype(BF16)
    ssm_w_out_b = ssm_w_out.astype(BF16)
    att_d = att_w_out.shape[1]
    qscale = jnp.concatenate([jnp.full((att_d,), ATT_HEADDIM ** -0.5 * LOG2E, F32), jnp.ones((att_d,), F32)])
    att_w_in_b = (att_w_in * qscale).astype(BF16)
    att_w_out_b = att_w_out.astype(BF16)
    w_kv_b = w_kv.astype(BF16)[None]

    (h,) = _resnorm(x, pre=(shift[0], scale[0], pre_norm_g[0]))
    kv = None
    for layer in range(depth):
        h2 = h.reshape(t, d)
        if layer < n_a:
            i = layer
            zx = _ssm_in_proj(h2, ssm_w_in_b, i, ssm_conv_w[i], ssm_conv_b[i], d_inner, s)
            dt = _matmul(h2, ssm_w_in_b, i, F32, 1024, LANES, "ssm_dt_proj", col0=zx_cols, ncols=nheads,
                         act="softplus", bias=ssm_dt_bias[i])
            yn = _ssd_scan(zx.reshape(bsz, s, zx_cols), dt.reshape(bsz, s, nheads), ssm_a_log[i], ssm_d[i],
                           ssm_norm_g[i])
            y = _matmul(yn.reshape(t, d_inner), ssm_w_out_b, i, BF16, 1024, 512, "ssm_out_proj")
        else:
            i = layer - n_a
            qz = _matmul(h2, att_w_in_b, i, BF16, 1024, 1024, "att_in_proj", act="silu", act_from=att_d // 1024)
            o = _attention(qz.reshape(bsz, s, 2 * att_d), kv, _rel_source_rows(att_rel_bias[i]))
            y = _matmul(o.reshape(t, att_d), att_w_out_b, i, BF16, 1024, 1024, "att_out_proj")
        post = (y.reshape(bsz, s, d), gate[layer], post_norm_g[layer])
        if layer + 1 < depth:
            nxt = layer + 1
            pre = (shift[nxt], scale[nxt], pre_norm_g[nxt])
            if nxt == n_a:
                x, h, hkv = _resnorm(x, post=post, pre=pre, kv_g=kv_norm_g)
                kv = _matmul(hkv.reshape(t, d), w_kv_b, 0, BF16, 1024, 1024, "kv_proj")
                kv = kv.reshape(bsz, s, w_kv.shape[1])
            else:
                x, h = _resnorm(x, post=post, pre=pre)
        else:
            (x,) = _resnorm(x, post=post)
    return x
```

```python
import functools

import jax
import jax.numpy as jnp
import numpy as np
from jax import lax
from jax.experimental import pallas as pl
from jax.experimental.pallas import tpu as pltpu

F32 = jnp.float32
BF16 = jnp.bfloat16

EPS = 1e-6
CHUNK = 64
LANES = 128
MXU_DIM = 256
V7X_VMEM_BYTES = 64 * 1024 * 1024
VMEM_LIMIT = V7X_VMEM_BYTES - 8 * 1024 * 1024

SSM_HEADDIM = 64
SSM_GROUPS = 8
SSM_STATE = 128
SSM_CONV = 4
ATT_HEADDIM = 128
ATT_LEFT_CHUNKS = 8
REL_MAX = 128
ATT_QBLK = 4 * CHUNK
ATT_KBLK = 3 * ATT_QBLK
ATT_HALF = ATT_QBLK // 2
ATT_WIN = ATT_KBLK - ATT_HALF
ATT_RVEC = 1024
ATT_VARIANTS = 3
LOG2E = 1.4426950408889634
NEG = -1e30


def _params(n_grid, vmem=VMEM_LIMIT):
    return pltpu.CompilerParams(dimension_semantics=("arbitrary",) * n_grid, vmem_limit_bytes=vmem)


def _silu(v):
    h = 0.5 * v
    return h + h * jnp.tanh(h)


def _softplus(v):
    return jnp.maximum(v, 0.0) + jnp.log1p(jnp.exp(-jnp.abs(v)))


def _ada_kernel(ct_ref, w_ref, b_ref, o_ref, sb_ref, *, nb, tn):
    d = ct_ref.shape[0]

    @pl.when((pl.program_id(0) == 0) & (pl.program_id(1) == 0))
    def _():
        s = _silu(ct_ref[...])
        for b in range(nb):
            sb_ref[b] = jnp.broadcast_to(s[:, b:b + 1], (d, LANES))

    rows = []
    for b in range(nb):
        cols = []
        for t in range(tn // LANES):
            w = w_ref[0, :, t * LANES:(t + 1) * LANES]
            cols.append(jnp.sum(w * sb_ref[b], axis=0, keepdims=True))
        rows.append(jnp.concatenate(cols, axis=1))
    o_ref[0] = jnp.concatenate(rows, axis=0) + b_ref[0]


def _ada_modulation(c, ada_w, ada_b, tn=512):
    depth, d, n3 = ada_w.shape
    nb = c.shape[0]
    return pl.pallas_call(
        functools.partial(_ada_kernel, nb=nb, tn=tn),
        grid=(depth, n3 // tn),
        in_specs=[pl.BlockSpec((d, nb), lambda l, j: (0, 0)),
                  pl.BlockSpec((1, d, tn), lambda l, j: (l, 0, j)),
                  pl.BlockSpec((1, 1, tn), lambda l, j: (l, 0, j))],
        out_specs=pl.BlockSpec((1, nb, tn), lambda l, j: (l, 0, j)),
        out_shape=jax.ShapeDtypeStruct((depth, nb, n3), F32),
        scratch_shapes=[pltpu.VMEM((nb, d, LANES), F32)],
        compiler_params=_params(2),
        name="ada_modulation",
    )(c.T, ada_w, ada_b.reshape(depth, 1, n3))


def _rms(v, g):
    var = jnp.mean(v * v, axis=-1, keepdims=True)
    return v * lax.rsqrt(var + EPS) * g


def _resnorm_kernel(*refs, has_post, has_pre, has_kv):
    it = iter(refs)
    x_ref = next(it)
    if has_post:
        y_ref, gate_ref, gpost_ref = next(it), next(it), next(it)
    if has_pre:
        shift_ref, scale_ref, gpre_ref = next(it), next(it), next(it)
    if has_kv:
        gkv_ref = next(it)
    x = x_ref[0]
    if has_post:
        x = x + gate_ref[0] * _rms(y_ref[0].astype(F32), gpost_ref[...])
        next(it)[0] = x
    if has_pre:
        h = _rms(x, gpre_ref[...]) * (1.0 + scale_ref[0]) + shift_ref[0]
        next(it)[0] = h.astype(BF16)
    if has_kv:
        next(it)[0] = _rms(x, gkv_ref[...]).astype(BF16)


def _resnorm(x, post=None, pre=None, kv_g=None, tr=256):
    bsz, s, d = x.shape
    row = pl.BlockSpec((1, tr, d), lambda b, i: (b, i, 0))
    per_b = pl.BlockSpec((1, 1, d), lambda b, i: (b, 0, 0))
    vec = pl.BlockSpec((1, d), lambda b, i: (0, 0))
    args, in_specs, out_specs, out_shape = [x], [row], [], []
    if post is not None:
        y, gate, g_post = post
        args += [y, gate, g_post.reshape(1, d)]
        in_specs += [row, per_b, vec]
        out_specs.append(row)
        out_shape.append(jax.ShapeDtypeStruct((bsz, s, d), F32))
    if pre is not None:
        shift, scale, g_pre = pre
        args += [shift, scale, g_pre.reshape(1, d)]
        in_specs += [per_b, per_b, vec]
        out_specs.append(row)
        out_shape.append(jax.ShapeDtypeStruct((bsz, s, d), BF16))
    if kv_g is not None:
        args.append(kv_g.reshape(1, d))
        in_specs.append(vec)
        out_specs.append(row)
        out_shape.append(jax.ShapeDtypeStruct((bsz, s, d), BF16))
    return pl.pallas_call(
        functools.partial(_resnorm_kernel, has_post=post is not None, has_pre=pre is not None,
                          has_kv=kv_g is not None),
        grid=(bsz, s // tr), in_specs=in_specs, out_specs=out_specs, out_shape=out_shape,
        compiler_params=_params(2), name="resnorm",
    )(*args)


def _mm_kernel(a_ref, b_ref, *rest):
    o_ref = rest[-1]
    tn = o_ref.shape[1]
    sub = min(MXU_DIM, tn)
    for c in range(tn // sub):
        cs = slice(c * sub, (c + 1) * sub)
        acc = jnp.dot(a_ref[...], b_ref[:, cs], preferred_element_type=F32)
        if len(rest) > 1:
            acc = _softplus(acc + rest[0][:, cs])
        o_ref[:, cs] = acc.astype(o_ref.dtype)


def _matmul(a, b, layer, out_dtype, tm, tn, name, col0=0, ncols=None, bias=None):
    m, k = a.shape
    n = b.shape[2] - col0 if ncols is None else ncols
    tm, tn = min(tm, m), min(tn, n)
    jb0 = col0 // tn
    args = [a, b]
    in_specs = [pl.BlockSpec((tm, k), lambda i, j: (i, 0)),
                pl.BlockSpec((None, k, tn), lambda i, j: (layer, 0, jb0 + j))]
    if bias is not None:
        args.append(bias.reshape(1, n))
        in_specs.append(pl.BlockSpec((1, tn), lambda i, j: (0, j)))
    return pl.pallas_call(
        _mm_kernel, grid=(m // tm, n // tn),
        in_specs=in_specs,
        out_specs=pl.BlockSpec((tm, tn), lambda i, j: (i, j)),
        out_shape=jax.ShapeDtypeStruct((m, n), out_dtype),
        compiler_params=_params(2), name=name,
    )(*args)


CONV_HALO = 8


def _ssm_in_kernel(a_ref, w_ref, cw_ref, cb_ref, o_ref, tail_ref, *, n_plain, tiles_per_seq):
    i, j = pl.program_id(0), pl.program_id(1)
    tm, tn = o_ref.shape
    sub = min(MXU_DIM, tn)

    @pl.when((i == 0) & (j == 0))
    def _():
        tail_ref[...] = jnp.zeros_like(tail_ref)

    def cols(c):
        return slice(c * sub, (c + 1) * sub)

    def sub_dots(epilogue):
        for c in range(tn // sub):
            epilogue(jnp.dot(a_ref[...], w_ref[:, cols(c)], preferred_element_type=F32), cols(c))

    @pl.when(j < n_plain)
    def _():
        def plain(acc, cs):
            o_ref[:, cs] = acc.astype(o_ref.dtype)
        sub_dots(plain)

    @pl.when(j >= n_plain)
    def _():
        jc = j - n_plain
        seq_start = i % tiles_per_seq == 0

        row8 = lax.broadcasted_iota(jnp.int32, (CONV_HALO, sub), 0)

        def conv(acc, cs):
            tail = jnp.where(seq_start, 0.0, tail_ref[jc, :, cs])
            tail_ref[jc, :, cs] = acc[tm - CONV_HALO:]
            w = cw_ref[:, cs]
            out = cb_ref[:, cs] + w[SSM_CONV - 1:SSM_CONV] * acc
            for d in range(1, SSM_CONV):
                sh = pltpu.roll(acc, d, 0)
                top = jnp.where(row8 < d, pltpu.roll(tail, d, 0), sh[:CONV_HALO])
                sh = jnp.concatenate([top, sh[CONV_HALO:]], axis=0)
                out = out + w[SSM_CONV - 1 - d:SSM_CONV - d] * sh
            o_ref[:, cs] = _silu(out).astype(o_ref.dtype)
        sub_dots(conv)


def _ssm_in_proj(h2, w_in, layer, conv_w, conv_b, n_plain_cols, seq, tm=1024, tn=1024):
    t, k = h2.shape
    c = conv_w.shape[1]
    n = n_plain_cols + c
    n_plain = n_plain_cols // tn
    conv_col = lambda i, j: (0, jnp.maximum(j - n_plain, 0))
    return pl.pallas_call(
        functools.partial(_ssm_in_kernel, n_plain=n_plain, tiles_per_seq=seq // tm),
        grid=(t // tm, n // tn),
        in_specs=[pl.BlockSpec((tm, k), lambda i, j: (i, 0)),
                  pl.BlockSpec((None, k, tn), lambda i, j: (layer, 0, j)),
                  pl.BlockSpec((SSM_CONV, tn), conv_col),
                  pl.BlockSpec((1, tn), conv_col)],
        out_specs=pl.BlockSpec((tm, tn), lambda i, j: (i, j)),
        out_shape=jax.ShapeDtypeStruct((t, n), BF16),
        scratch_shapes=[pltpu.VMEM((c // tn, CONV_HALO, tn), F32)],
        compiler_params=_params(2), name="ssm_in_proj",
    )(h2, w_in, conv_w, conv_b.reshape(1, c))


def _scan_kernel(x_ref, b_ref, c_ref, z_ref, dta_ref, dtt_ref, alr_ref, alc_ref,
                 dexp_ref, ng_ref, ex_ref, o_ref, st_ref, *, nsub):
    L = CHUNK
    P = SSM_HEADDIM
    hp = LANES // P
    nheads = alr_ref.shape[2]
    npair = nheads // hp
    qh = MXU_DIM // P
    nquad = nheads // qh

    @pl.when(pl.program_id(2) == 0)
    def _():
        st_ref[...] = jnp.zeros_like(st_ref)

    a_r = -jnp.exp(alr_ref[0])
    a_c = -jnp.exp(alc_ref[0])
    lane = lax.broadcasted_iota(jnp.int32, (L, LANES), 1)
    sub = lax.broadcasted_iota(jnp.int32, (L, LANES), 0)
    lo = lane < P
    s_of_lane = lane & (P - 1)
    causal2 = s_of_lane <= sub
    tri = (lax.broadcasted_iota(jnp.int32, (L, L), 1) <= lax.broadcasted_iota(jnp.int32, (L, L), 0)).astype(BF16)
    tri_t2 = (sub <= s_of_lane).astype(BF16)
    blk = lax.broadcasted_iota(jnp.int32, (L, qh * P), 1) // P
    dexp = dexp_ref[...]
    ng = ng_ref[...]

    def pieces(v, n=3):
        out, rem = [], v
        for _ in range(n):
            hi = rem.astype(BF16).astype(F32)
            out.append(hi)
            rem = rem - hi
        return out

    def local(ci):
        rows = slice(ci * L, (ci + 1) * L)
        x = x_ref[0, rows, :]
        bm = b_ref[0, rows, :]
        cm = c_ref[0, rows, :]
        dt = dta_ref[0, ci, 0]
        dt_t = dtt_ref[0, ci, 0]
        acum = sum(jnp.dot(tri, p.astype(BF16), preferred_element_type=F32) for p in pieces(dt * a_r))
        acum_t = sum(jnp.dot(p.astype(BF16), tri_t2, preferred_element_type=F32)
                     for p in pieces(dt_t[:, :L] * a_c))
        e_in = jnp.exp(acum)
        e_out = jnp.exp(acum[L - 1:L, :] - acum) * dt
        ex = ex_ref[...]
        spr = lambda v, n: jnp.dot(jnp.concatenate(pieces(v, n), axis=1).astype(BF16), ex[:n * nheads],
                                   preferred_element_type=F32)
        col, sc_in, sc_out = spr(acum, 3), spr(e_in, 2), spr(e_out, 2)
        cb2 = lax.dot_general(cm, jnp.concatenate([bm, bm], axis=0), (((1,), (1,)), ((), ())),
                              preferred_element_type=F32)
        xf = x.astype(F32)

        w_tiles = []
        for j in range(npair):
            r0, r1 = hp * j, hp * j + 1
            row = jnp.where(lo[0:1], acum_t[r0:r0 + 1], acum_t[r1:r1 + 1])
            dtr = jnp.where(lo[0:1], dt_t[r0:r0 + 1], dt_t[r1:r1 + 1])
            seg = jnp.where(causal2, col[:, j * LANES:(j + 1) * LANES] - row, NEG)
            w_tiles.append((jnp.exp(seg) * (cb2 * dtr)).astype(BF16))

        y_parts = []
        for q in range(nquad):
            w4 = jnp.concatenate(w_tiles[q * qh // hp:(q + 1) * qh // hp], axis=1)
            x4 = x[:, q * qh * P:(q + 1) * qh * P]
            xbd = jnp.concatenate([jnp.where(blk == i, x4, jnp.zeros_like(x4)) for i in range(qh)], axis=0)
            y_parts.append(jnp.dot(w4, xbd, preferred_element_type=F32))
        y_loc = jnp.concatenate(y_parts, axis=1) + xf * dexp

        xw = x * sc_out.astype(BF16)
        upd = lax.dot_general(bm, xw, (((0,), (0,)), ((), ())), preferred_element_type=F32)
        return cm, sc_in, y_loc, upd

    def carried(ci, st, cm, sc_in, y_loc, upd):
        rows = slice(ci * L, (ci + 1) * L)
        y = y_loc + jnp.dot(cm, st.astype(BF16), preferred_element_type=F32) * sc_in
        yg = y * _silu(z_ref[0, rows, :].astype(F32))
        o_ref[0, rows, :] = _rms(yg, ng).astype(o_ref.dtype)
        return st * sc_in[L - 1:L, :] + upd

    st = st_ref[...]
    for ci in range(nsub):
        st = carried(ci, st, *local(ci))
    st_ref[...] = st


def _spread_matrix(nheads_g, parts=3):
    rows = np.arange(parts * nheads_g)
    cols = np.arange(nheads_g * SSM_HEADDIM)
    return jnp.asarray((rows % nheads_g)[:, None] == (cols // SSM_HEADDIM)[None, :], dtype=BF16)


def _ssd_scan(zx, dt, a_log, d_skip, norm_g, nsub=16):
    bsz, s, _ = zx.shape
    g = SSM_GROUPS
    nheads = dt.shape[-1]
    r = nheads // g
    gw = r * SSM_HEADDIM
    d_inner = nheads * SSM_HEADDIM
    nc = s // CHUNK
    lb = nsub * CHUNK
    dt5 = dt.reshape(bsz, nc, CHUNK, g, r)
    dt_a = dt5.transpose(0, 1, 3, 2, 4)
    dt_t = dt5.transpose(0, 1, 3, 4, 2)
    dt_t = jnp.concatenate([dt_t, dt_t], axis=-1)
    x_off = d_inner // gw
    b_off = 2 * d_inner // SSM_STATE
    spread = _spread_matrix(r)
    row_spec = pl.BlockSpec((1, 1, r), lambda b, gi, ci: (gi, 0, 0))
    col_spec = pl.BlockSpec((1, r, 1), lambda b, gi, ci: (gi, 0, 0))
    wide = lambda off: pl.BlockSpec((1, lb, gw), lambda b, gi, ci: (b, ci, off + gi))
    narrow = lambda off: pl.BlockSpec((1, lb, SSM_STATE), lambda b, gi, ci: (b, ci, off + gi))
    return pl.pallas_call(
        functools.partial(_scan_kernel, nsub=nsub),
        grid=(bsz, g, nc // nsub),
        in_specs=[wide(x_off), narrow(b_off), narrow(b_off + g), wide(0),
                  pl.BlockSpec((1, nsub, 1, CHUNK, r), lambda b, gi, ci: (b, ci, gi, 0, 0)),
                  pl.BlockSpec((1, nsub, 1, r, 2 * CHUNK), lambda b, gi, ci: (b, ci, gi, 0, 0)),
                  row_spec, col_spec,
                  pl.BlockSpec((1, gw), lambda b, gi, ci: (0, gi)),
                  pl.BlockSpec((1, gw), lambda b, gi, ci: (0, gi)),
                  pl.BlockSpec(spread.shape, lambda b, gi, ci: (0, 0))],
        out_specs=wide(0),
        out_shape=jax.ShapeDtypeStruct((bsz, s, d_inner), BF16),
        scratch_shapes=[pltpu.VMEM((SSM_STATE, gw), F32)],
        compiler_params=_params(3), name="ssd_scan",
    )(zx, zx, zx, zx, dt_a, dt_t, a_log.reshape(g, 1, r), a_log.reshape(g, r, 1),
      jnp.repeat(d_skip, SSM_HEADDIM).reshape(1, d_inner), norm_g.reshape(1, d_inner), spread)


def _attn_kernel(q_ref, z_ref, k0_ref, k1_ref, k2_ref, v0_ref, v1_ref, v2_ref, rv_ref, o_ref, bias_sc, *, hb):
    i = pl.program_id(2)

    @pl.when((pl.program_id(1) == 0) & (i == 0))
    def _():
        l = lax.broadcasted_iota(jnp.int32, (ATT_QBLK, ATT_KBLK), 0)
        j = lax.broadcasted_iota(jnp.int32, (ATT_QBLK, ATT_KBLK), 1)
        lc, jc = l // CHUNK, j // CHUNK
        band = (jc >= lc) & (jc <= lc + ATT_LEFT_CHUNKS)
        for h in range(hb):
            src = jnp.broadcast_to(rv_ref[h], (ATT_QBLK, ATT_RVEC))
            toe = pltpu.roll(src, ATT_RVEC - ATT_QBLK, 1, stride=1, stride_axis=0)[:, :ATT_KBLK]
            for v in range(ATT_VARIANTS):
                ok = band & (j >= (ATT_KBLK - ATT_QBLK) - ATT_QBLK * v)
                b = jnp.where(ok, toe, NEG)
                for half in range(2):
                    bias_sc[v, h, half] = b[half * ATT_HALF:(half + 1) * ATT_HALF,
                                            half * ATT_HALF:half * ATT_HALF + ATT_WIN]

    v = jnp.minimum(i, ATT_VARIANTS - 1)
    units = [(h, half) for h in range(hb) for half in range(2)]

    def lanes(h):
        return slice(h * ATT_HEADDIM, (h + 1) * ATT_HEADDIM)

    def rows(half):
        return slice(half * ATT_HALF, (half + 1) * ATT_HALF)

    def window(refs, h, half):
        cat = jnp.concatenate([r[0, :, lanes(h)] for r in refs], axis=0)
        return cat[half * ATT_HALF:half * ATT_HALF + ATT_WIN]

    def scores(h, half):
        s = lax.dot_general(q_ref[0, rows(half), lanes(h)], window((k0_ref, k1_ref, k2_ref), h, half),
                            (((1,), (1,)), ((), ())), preferred_element_type=F32)
        return s + bias_sc[v, h, half]

    def probs(s):
        p = jnp.exp2(s - jnp.max(s, axis=1, keepdims=True))
        return p.astype(BF16), jnp.sum(p, axis=1, keepdims=True)

    def finish(h, half, p, den):
        o = jnp.dot(p, window((v0_ref, v1_ref, v2_ref), h, half), preferred_element_type=F32) / den
        o_ref[0, rows(half), lanes(h)] = (o * _silu(z_ref[0, rows(half), lanes(h)].astype(F32))).astype(o_ref.dtype)

    stages = [lambda u, _: scores(*units[u]), lambda u, s: probs(s), lambda u, pd: finish(*units[u], *pd)]
    inflight = {}
    n = len(units)
    for step in range(n + len(stages) - 1):
        for k, stage in enumerate(stages):
            u = step - k
            if 0 <= u < n:
                inflight[u] = stage(u, inflight.get(u))


def _rel_source_rows(rel_table):
    u = np.arange(ATT_RVEC)
    bucket = np.clip(ATT_KBLK - u, -REL_MAX, REL_MAX) + REL_MAX
    return (rel_table[:, bucket].astype(F32) * LOG2E)[:, None, :]


def _attention(qz, kv, rel_rows, hb=8):
    bsz, s, two_d = qz.shape
    d = two_d // 2
    hw = hb * ATT_HEADDIM
    ng = d // hw
    cur = lambda off: pl.BlockSpec((1, ATT_QBLK, hw), lambda hg, b, i: (b, i, off + hg))
    back = lambda off, n: pl.BlockSpec((1, ATT_QBLK, hw), lambda hg, b, i: (b, jnp.maximum(i - n, 0), off + hg))
    return pl.pallas_call(
        functools.partial(_attn_kernel, hb=hb),
        grid=(ng, bsz, s // ATT_QBLK),
        in_specs=[cur(0), cur(ng), back(0, 2), back(0, 1), cur(0), back(ng, 2), back(ng, 1), cur(ng),
                  pl.BlockSpec((hb, 1, ATT_RVEC), lambda hg, b, i: (hg, 0, 0))],
        out_specs=pl.BlockSpec((1, ATT_QBLK, hw), lambda hg, b, i: (b, i, hg)),
        out_shape=jax.ShapeDtypeStruct((bsz, s, d), BF16),
        scratch_shapes=[pltpu.VMEM((ATT_VARIANTS, hb, 2, ATT_HALF, ATT_WIN), F32)],
        compiler_params=_params(3), name="band_attention",
    )(qz, qz, kv, kv, kv, kv, kv, kv, rel_rows)


def kernel(x, c, ada_w, ada_b, pre_norm_g, post_norm_g, ssm_w_in, ssm_conv_w, ssm_conv_b, ssm_dt_bias, ssm_a_log, ssm_d, ssm_norm_g, ssm_w_out, kv_norm_g, w_kv, att_w_in, att_rel_bias, att_w_out):
    bsz, s, d = x.shape
    depth = ada_w.shape[0]
    n_a = ssm_w_in.shape[0]
    t = bsz * s
    d_inner = ssm_w_out.shape[1]
    nheads = ssm_dt_bias.shape[1]
    zx_cols = ssm_w_in.shape[2] - nheads

    mod = _ada_modulation(c, ada_w, ada_b)
    shift = [mod[l, :, None, 0:d] for l in range(depth)]
    scale = [mod[l, :, None, d:2 * d] for l in range(depth)]
    gate = [mod[l, :, None, 2 * d:3 * d] for l in range(depth)]

    ssm_w_in_b = ssm_w_in.astype(BF16)
    ssm_w_out_b = ssm_w_out.astype(BF16)
    att_d = att_w_out.shape[1]
    qscale = jnp.concatenate([jnp.full((att_d,), ATT_HEADDIM ** -0.5 * LOG2E, F32), jnp.ones((att_d,), F32)])
    att_w_in_b = (att_w_in * qscale).astype(BF16)
    att_w_out_b = att_w_out.astype(BF16)
    w_kv_b = w_kv.astype(BF16)[None]

    (h,) = _resnorm(x, pre=(shift[0], scale[0], pre_norm_g[0]))
    kv = None
    for layer in range(depth):
        h2 = h.reshape(t, d)
        if layer < n_a:
            i = layer
            zx = _ssm_in_proj(h2, ssm_w_in_b, i, ssm_conv_w[i], ssm_conv_b[i], d_inner, s)
            dt = _matmul(h2, ssm_w_in_b, i, F32, 1024, LANES, "ssm_dt_proj", col0=zx_cols, ncols=nheads,
                         bias=ssm_dt_bias[i])
            yn = _ssd_scan(zx.reshape(bsz, s, zx_cols), dt.reshape(bsz, s, nheads), ssm_a_log[i], ssm_d[i],
                           ssm_norm_g[i])
            y = _matmul(yn.reshape(t, d_inner), ssm_w_out_b, i, BF16, 1024, 512, "ssm_out_proj")
        else:
            i = layer - n_a
            qz = _matmul(h2, att_w_in_b, i, BF16, 1024, 1024, "att_in_proj")
            o = _attention(qz.reshape(bsz, s, 2 * att_d), kv, _rel_source_rows(att_rel_bias[i]))
            y = _matmul(o.reshape(t, att_d), att_w_out_b, i, BF16, 1024, 1024, "att_out_proj")
        post = (y.reshape(bsz, s, d), gate[layer], post_norm_g[layer])
        if layer + 1 < depth:
            nxt = layer + 1
            pre = (shift[nxt], scale[nxt], pre_norm_g[nxt])
            if nxt == n_a:
                x, h, hkv = _resnorm(x, post=post, pre=pre, kv_g=kv_norm_g)
                kv = _matmul(hkv.reshape(t, d), w_kv_b, 0, BF16, 1024, 1024, "kv_proj")
                kv = kv.reshape(bsz, s, w_kv.shape[1])
            else:
                x, h = _resnorm(x, post=post, pre=pre)
        else:
            (x,) = _resnorm(x, post=post)
    return x
```

```python
import functools
from typing import NamedTuple, Optional

import jax
import jax.numpy as jnp
import numpy as np
from jax import lax
from jax.experimental import pallas as pl
from jax.experimental.pallas import tpu as pltpu

F32 = jnp.float32
BF16 = jnp.bfloat16

EPS = 1e-6
CHUNK = 64
LANES = 128
MXU_DIM = 256
V7X_VMEM_BYTES = 64 * 1024 * 1024
VMEM_LIMIT = V7X_VMEM_BYTES - 8 * 1024 * 1024

SSM_HEADDIM = 64
SSM_GROUPS = 8
SSM_STATE = 128
SSM_CONV = 4
ATT_HEADDIM = 128
ATT_LEFT_CHUNKS = 8
REL_MAX = 128
ATT_QBLK = 4 * CHUNK
ATT_KBLK = 3 * ATT_QBLK
ATT_HALF = ATT_QBLK // 2
ATT_WIN = ATT_KBLK - ATT_HALF
ATT_RVEC = 1024
ATT_VARIANTS = 3
LOG2E = 1.4426950408889634
NEG = -1e30
SIDE_JOB_BLOCKS = 256


def _params(n_grid, vmem=VMEM_LIMIT):
    return pltpu.CompilerParams(dimension_semantics=("arbitrary",) * n_grid, vmem_limit_bytes=vmem)


def _silu(v):
    h = 0.5 * v
    return h + h * jnp.tanh(h)


def _softplus(v):
    return jnp.maximum(v, 0.0) + jnp.log1p(jnp.exp(-jnp.abs(v)))


def _ada_kernel(ct_ref, w_ref, b_ref, o_ref, sb_ref, *, nb, tn):
    d = ct_ref.shape[0]

    @pl.when((pl.program_id(0) == 0) & (pl.program_id(1) == 0))
    def _():
        s = _silu(ct_ref[...])
        for b in range(nb):
            sb_ref[b] = jnp.broadcast_to(s[:, b:b + 1], (d, LANES))

    rows = []
    for b in range(nb):
        cols = []
        for t in range(tn // LANES):
            w = w_ref[0, :, t * LANES:(t + 1) * LANES]
            cols.append(jnp.sum(w * sb_ref[b], axis=0, keepdims=True))
        rows.append(jnp.concatenate(cols, axis=1))
    o_ref[0] = jnp.concatenate(rows, axis=0) + b_ref[0]


def _ada_modulation(c, ada_w, ada_b, tn=512):
    depth, d, n3 = ada_w.shape
    nb = c.shape[0]
    return pl.pallas_call(
        functools.partial(_ada_kernel, nb=nb, tn=tn),
        grid=(depth, n3 // tn),
        in_specs=[pl.BlockSpec((d, nb), lambda l, j: (0, 0)),
                  pl.BlockSpec((1, d, tn), lambda l, j: (l, 0, j)),
                  pl.BlockSpec((1, 1, tn), lambda l, j: (l, 0, j))],
        out_specs=pl.BlockSpec((1, nb, tn), lambda l, j: (l, 0, j)),
        out_shape=jax.ShapeDtypeStruct((depth, nb, n3), F32),
        scratch_shapes=[pltpu.VMEM((nb, d, LANES), F32)],
        compiler_params=_params(2),
        name="ada_modulation",
    )(c.T, ada_w, ada_b.reshape(depth, 1, n3))


def _rms(v, g):
    var = jnp.mean(v * v, axis=-1, keepdims=True)
    return v * lax.rsqrt(var + EPS) * g


def _resnorm_kernel(*refs, has_post, has_pre, has_kv):
    it = iter(refs)
    x_ref = next(it)
    if has_post:
        y_ref, gate_ref, gpost_ref = next(it), next(it), next(it)
    if has_pre:
        shift_ref, scale_ref, gpre_ref = next(it), next(it), next(it)
    if has_kv:
        gkv_ref = next(it)
    x = x_ref[0]
    if has_post:
        x = x + gate_ref[0] * _rms(y_ref[0].astype(F32), gpost_ref[...])
        next(it)[0] = x
    if has_pre:
        h = _rms(x, gpre_ref[...]) * (1.0 + scale_ref[0]) + shift_ref[0]
        next(it)[0] = h.astype(BF16)
    if has_kv:
        next(it)[0] = _rms(x, gkv_ref[...]).astype(BF16)


def _resnorm(x, post=None, pre=None, kv_g=None, tr=256):
    bsz, s, d = x.shape
    row = pl.BlockSpec((1, tr, d), lambda b, i: (b, i, 0))
    per_b = pl.BlockSpec((1, 1, d), lambda b, i: (b, 0, 0))
    vec = pl.BlockSpec((1, d), lambda b, i: (0, 0))
    args, in_specs, out_specs, out_shape = [x], [row], [], []
    if post is not None:
        y, gate, g_post = post
        args += [y, gate, g_post.reshape(1, d)]
        in_specs += [row, per_b, vec]
        out_specs.append(row)
        out_shape.append(jax.ShapeDtypeStruct((bsz, s, d), F32))
    if pre is not None:
        shift, scale, g_pre = pre
        args += [shift, scale, g_pre.reshape(1, d)]
        in_specs += [per_b, per_b, vec]
        out_specs.append(row)
        out_shape.append(jax.ShapeDtypeStruct((bsz, s, d), BF16))
    if kv_g is not None:
        args.append(kv_g.reshape(1, d))
        in_specs.append(vec)
        out_specs.append(row)
        out_shape.append(jax.ShapeDtypeStruct((bsz, s, d), BF16))
    return pl.pallas_call(
        functools.partial(_resnorm_kernel, has_post=post is not None, has_pre=pre is not None,
                          has_kv=kv_g is not None),
        grid=(bsz, s // tr), in_specs=in_specs, out_specs=out_specs, out_shape=out_shape,
        compiler_params=_params(2), name="resnorm",
    )(*args)


class _CastJob(NamedTuple):
    w: jax.Array
    layer: int
    scale: Optional[jax.Array]


def _job_specs(jobs, ni, nj):
    nblk = min(SIDE_JOB_BLOCKS, 1 << ((ni * nj).bit_length() - 1))
    args, in_specs, out_specs, out_shape = [], [], [], []
    for job in jobs:
        _, r, c = job.w.shape
        rb = r // nblk
        blk = lambda i, j: jnp.minimum(i * nj + j, nblk - 1)
        args.append(job.w)
        in_specs.append(pl.BlockSpec((None, rb, c), lambda i, j, l=job.layer: (l, blk(i, j), 0)))
        if job.scale is not None:
            args.append(job.scale.reshape(1, c))
            in_specs.append(pl.BlockSpec((1, c), lambda i, j: (0, 0)))
        out_specs.append(pl.BlockSpec((rb, c), lambda i, j: (blk(i, j), 0)))
        out_shape.append(jax.ShapeDtypeStruct((r, c), BF16))
    return args, in_specs, out_specs, out_shape


def _run_cast_jobs(in_refs, out_refs, scaled):
    it = iter(in_refs)
    for o_ref, has_scale in zip(out_refs, scaled):
        w = next(it)[...]
        if has_scale:
            w = w * next(it)[...]
        o_ref[...] = w.astype(o_ref.dtype)


def _n_job_inputs(scaled):
    return len(scaled) + sum(scaled)


def _mm_kernel(*refs, has_bias, scaled):
    n_in = 2 + has_bias + _n_job_inputs(scaled)
    a_ref, b_ref = refs[:2]
    o_ref = refs[n_in]
    _run_cast_jobs(refs[2 + has_bias:n_in], refs[n_in + 1:], scaled)
    tn = o_ref.shape[1]
    sub = min(MXU_DIM, tn)
    for c in range(tn // sub):
        cs = slice(c * sub, (c + 1) * sub)
        acc = jnp.dot(a_ref[...], b_ref[:, cs], preferred_element_type=F32)
        if has_bias:
            acc = _softplus(acc + refs[2][:, cs])
        o_ref[:, cs] = acc.astype(o_ref.dtype)


def _matmul(a, b, layer, out_dtype, tm, tn, name, col0=0, ncols=None, bias=None, jobs=()):
    m, k = a.shape
    n = b.shape[2] - col0 if ncols is None else ncols
    tm, tn = min(tm, m), min(tn, n)
    jb0 = col0 // tn
    args = [a, b]
    in_specs = [pl.BlockSpec((tm, k), lambda i, j: (i, 0)),
                pl.BlockSpec((None, k, tn), lambda i, j: (layer, 0, jb0 + j))]
    if bias is not None:
        args.append(bias.reshape(1, n))
        in_specs.append(pl.BlockSpec((1, tn), lambda i, j: (0, j)))
    j_args, j_in, j_out, j_shape = _job_specs(jobs, m // tm, n // tn)
    out = pl.pallas_call(
        functools.partial(_mm_kernel, has_bias=bias is not None, scaled=tuple(jb.scale is not None for jb in jobs)),
        grid=(m // tm, n // tn),
        in_specs=in_specs + j_in,
        out_specs=[pl.BlockSpec((tm, tn), lambda i, j: (i, j))] + j_out,
        out_shape=[jax.ShapeDtypeStruct((m, n), out_dtype)] + j_shape,
        compiler_params=_params(2), name=name,
    )(*args, *j_args)
    return out if jobs else out[0]


CONV_HALO = 8


def _ssm_in_kernel(*refs, n_plain, tiles_per_seq, scaled):
    n_in = 4 + _n_job_inputs(scaled)
    a_ref, w_ref, cw_ref, cb_ref = refs[:4]
    o_ref, tail_ref = refs[n_in], refs[-1]
    _run_cast_jobs(refs[4:n_in], refs[n_in + 1:-1], scaled)
    i, j = pl.program_id(0), pl.program_id(1)
    tm, tn = o_ref.shape
    sub = min(MXU_DIM, tn)

    @pl.when((i == 0) & (j == 0))
    def _():
        tail_ref[...] = jnp.zeros_like(tail_ref)

    def cols(c):
        return slice(c * sub, (c + 1) * sub)

    def sub_dots(epilogue):
        for c in range(tn // sub):
            epilogue(jnp.dot(a_ref[...], w_ref[:, cols(c)], preferred_element_type=F32), cols(c))

    @pl.when(j < n_plain)
    def _():
        def plain(acc, cs):
            o_ref[:, cs] = acc.astype(o_ref.dtype)
        sub_dots(plain)

    @pl.when(j >= n_plain)
    def _():
        jc = j - n_plain
        seq_start = i % tiles_per_seq == 0
        row8 = lax.broadcasted_iota(jnp.int32, (CONV_HALO, sub), 0)

        def conv(acc, cs):
            tail = jnp.where(seq_start, 0.0, tail_ref[jc, :, cs])
            tail_ref[jc, :, cs] = acc[tm - CONV_HALO:]
            w = cw_ref[:, cs]
            out = cb_ref[:, cs] + w[SSM_CONV - 1:SSM_CONV] * acc
            for d in range(1, SSM_CONV):
                sh = pltpu.roll(acc, d, 0)
                top = jnp.where(row8 < d, pltpu.roll(tail, d, 0), sh[:CONV_HALO])
                sh = jnp.concatenate([top, sh[CONV_HALO:]], axis=0)
                out = out + w[SSM_CONV - 1 - d:SSM_CONV - d] * sh
            o_ref[:, cs] = _silu(out).astype(o_ref.dtype)
        sub_dots(conv)


def _ssm_in_proj(h2, w_in, layer, conv_w, conv_b, n_plain_cols, seq, jobs=(), tm=1024, tn=1024):
    t, k = h2.shape
    c = conv_w.shape[1]
    n = n_plain_cols + c
    n_plain = n_plain_cols // tn
    conv_col = lambda i, j: (0, jnp.maximum(j - n_plain, 0))
    j_args, j_in, j_out, j_shape = _job_specs(jobs, t // tm, n // tn)
    return pl.pallas_call(
        functools.partial(_ssm_in_kernel, n_plain=n_plain, tiles_per_seq=seq // tm,
                          scaled=tuple(jb.scale is not None for jb in jobs)),
        grid=(t // tm, n // tn),
        in_specs=[pl.BlockSpec((tm, k), lambda i, j: (i, 0)),
                  pl.BlockSpec((None, k, tn), lambda i, j: (layer, 0, j)),
                  pl.BlockSpec((SSM_CONV, tn), conv_col),
                  pl.BlockSpec((1, tn), conv_col)] + j_in,
        out_specs=[pl.BlockSpec((tm, tn), lambda i, j: (i, j))] + j_out,
        out_shape=[jax.ShapeDtypeStruct((t, n), BF16)] + j_shape,
        scratch_shapes=[pltpu.VMEM((c // tn, CONV_HALO, tn), F32)],
        compiler_params=_params(2), name="ssm_in_proj",
    )(h2, w_in, conv_w, conv_b.reshape(1, c), *j_args)


def _scan_kernel(x_ref, b_ref, c_ref, z_ref, dta_ref, dtt_ref, alr_ref, alc_ref,
                 dexp_ref, ng_ref, ex_ref, o_ref, st_ref, *, nsub):
    L = CHUNK
    P = SSM_HEADDIM
    hp = LANES // P
    nheads = alr_ref.shape[2]
    npair = nheads // hp
    qh = MXU_DIM // P
    nquad = nheads // qh

    @pl.when(pl.program_id(2) == 0)
    def _():
        st_ref[...] = jnp.zeros_like(st_ref)

    a_r = -jnp.exp(alr_ref[0])
    a_c = -jnp.exp(alc_ref[0])
    lane = lax.broadcasted_iota(jnp.int32, (L, LANES), 1)
    sub = lax.broadcasted_iota(jnp.int32, (L, LANES), 0)
    lo = lane < P
    s_of_lane = lane & (P - 1)
    causal2 = s_of_lane <= sub
    tri = (lax.broadcasted_iota(jnp.int32, (L, L), 1) <= lax.broadcasted_iota(jnp.int32, (L, L), 0)).astype(BF16)
    tri_t2 = (sub <= s_of_lane).astype(BF16)
    blk = lax.broadcasted_iota(jnp.int32, (L, qh * P), 1) // P
    dexp = dexp_ref[...]
    ng = ng_ref[...]

    def pieces(v, n=3):
        out, rem = [], v
        for _ in range(n):
            hi = rem.astype(BF16).astype(F32)
            out.append(hi)
            rem = rem - hi
        return out

    def local(ci):
        rows = slice(ci * L, (ci + 1) * L)
        x = x_ref[0, rows, :]
        bm = b_ref[0, rows, :]
        cm = c_ref[0, rows, :]
        dt = dta_ref[0, ci, 0]
        dt_t = dtt_ref[0, ci, 0]
        acum = sum(jnp.dot(tri, p.astype(BF16), preferred_element_type=F32) for p in pieces(dt * a_r))
        acum_t = sum(jnp.dot(p.astype(BF16), tri_t2, preferred_element_type=F32)
                     for p in pieces(dt_t[:, :L] * a_c))
        e_in = jnp.exp(acum)
        e_out = jnp.exp(acum[L - 1:L, :] - acum) * dt
        ex = ex_ref[...]
        spr = lambda v, n: jnp.dot(jnp.concatenate(pieces(v, n), axis=1).astype(BF16), ex[:n * nheads],
                                   preferred_element_type=F32)
        col, sc_in, sc_out = spr(acum, 3), spr(e_in, 2), spr(e_out, 2)
        cb2 = lax.dot_general(cm, jnp.concatenate([bm, bm], axis=0), (((1,), (1,)), ((), ())),
                              preferred_element_type=F32)
        xf = x.astype(F32)

        w_tiles = []
        for j in range(npair):
            r0, r1 = hp * j, hp * j + 1
            row = jnp.where(lo[0:1], acum_t[r0:r0 + 1], acum_t[r1:r1 + 1])
            dtr = jnp.where(lo[0:1], dt_t[r0:r0 + 1], dt_t[r1:r1 + 1])
            seg = jnp.where(causal2, col[:, j * LANES:(j + 1) * LANES] - row, NEG)
            w_tiles.append((jnp.exp(seg) * (cb2 * dtr)).astype(BF16))

        y_parts = []
        for q in range(nquad):
            w4 = jnp.concatenate(w_tiles[q * qh // hp:(q + 1) * qh // hp], axis=1)
            x4 = x[:, q * qh * P:(q + 1) * qh * P]
            xbd = jnp.concatenate([jnp.where(blk == i, x4, jnp.zeros_like(x4)) for i in range(qh)], axis=0)
            y_parts.append(jnp.dot(w4, xbd, preferred_element_type=F32))
        y_loc = jnp.concatenate(y_parts, axis=1) + xf * dexp

        xw = x * sc_out.astype(BF16)
        upd = lax.dot_general(bm, xw, (((0,), (0,)), ((), ())), preferred_element_type=F32)
        return cm, sc_in, y_loc, upd

    def carried(ci, st, cm, sc_in, y_loc, upd):
        rows = slice(ci * L, (ci + 1) * L)
        y = y_loc + jnp.dot(cm, st.astype(BF16), preferred_element_type=F32) * sc_in
        yg = y * _silu(z_ref[0, rows, :].astype(F32))
        o_ref[0, rows, :] = _rms(yg, ng).astype(o_ref.dtype)
        return st * sc_in[L - 1:L, :] + upd

    st = st_ref[...]
    for ci in range(nsub):
        st = carried(ci, st, *local(ci))
    st_ref[...] = st


def _spread_matrix(nheads_g, parts=3):
    rows = np.arange(parts * nheads_g)
    cols = np.arange(nheads_g * SSM_HEADDIM)
    return jnp.asarray((rows % nheads_g)[:, None] == (cols // SSM_HEADDIM)[None, :], dtype=BF16)


def _ssd_scan(zx, dt, a_log, d_skip, norm_g, nsub=16):
    bsz, s, _ = zx.shape
    g = SSM_GROUPS
    nheads = dt.shape[-1]
    r = nheads // g
    gw = r * SSM_HEADDIM
    d_inner = nheads * SSM_HEADDIM
    nc = s // CHUNK
    lb = nsub * CHUNK
    dt5 = dt.reshape(bsz, nc, CHUNK, g, r)
    dt_a = dt5.transpose(0, 1, 3, 2, 4)
    dt_t = dt5.transpose(0, 1, 3, 4, 2)
    dt_t = jnp.concatenate([dt_t, dt_t], axis=-1)
    x_off = d_inner // gw
    b_off = 2 * d_inner // SSM_STATE
    spread = _spread_matrix(r)
    row_spec = pl.BlockSpec((1, 1, r), lambda b, gi, ci: (gi, 0, 0))
    col_spec = pl.BlockSpec((1, r, 1), lambda b, gi, ci: (gi, 0, 0))
    wide = lambda off: pl.BlockSpec((1, lb, gw), lambda b, gi, ci: (b, ci, off + gi))
    narrow = lambda off: pl.BlockSpec((1, lb, SSM_STATE), lambda b, gi, ci: (b, ci, off + gi))
    return pl.pallas_call(
        functools.partial(_scan_kernel, nsub=nsub),
        grid=(bsz, g, nc // nsub),
        in_specs=[wide(x_off), narrow(b_off), narrow(b_off + g), wide(0),
                  pl.BlockSpec((1, nsub, 1, CHUNK, r), lambda b, gi, ci: (b, ci, gi, 0, 0)),
                  pl.BlockSpec((1, nsub, 1, r, 2 * CHUNK), lambda b, gi, ci: (b, ci, gi, 0, 0)),
                  row_spec, col_spec,
                  pl.BlockSpec((1, gw), lambda b, gi, ci: (0, gi)),
                  pl.BlockSpec((1, gw), lambda b, gi, ci: (0, gi)),
                  pl.BlockSpec(spread.shape, lambda b, gi, ci: (0, 0))],
        out_specs=wide(0),
        out_shape=jax.ShapeDtypeStruct((bsz, s, d_inner), BF16),
        scratch_shapes=[pltpu.VMEM((SSM_STATE, gw), F32)],
        compiler_params=_params(3), name="ssd_scan",
    )(zx, zx, zx, zx, dt_a, dt_t, a_log.reshape(g, 1, r), a_log.reshape(g, r, 1),
      jnp.repeat(d_skip, SSM_HEADDIM).reshape(1, d_inner), norm_g.reshape(1, d_inner), spread)


def _attn_kernel(q_ref, z_ref, k0_ref, k1_ref, k2_ref, v0_ref, v1_ref, v2_ref, rv_ref, o_ref, bias_sc, *, hb):
    i = pl.program_id(2)

    @pl.when((pl.program_id(1) == 0) & (i == 0))
    def _():
        l = lax.broadcasted_iota(jnp.int32, (ATT_QBLK, ATT_KBLK), 0)
        j = lax.broadcasted_iota(jnp.int32, (ATT_QBLK, ATT_KBLK), 1)
        lc, jc = l // CHUNK, j // CHUNK
        band = (jc >= lc) & (jc <= lc + ATT_LEFT_CHUNKS)
        for h in range(hb):
            src = jnp.broadcast_to(rv_ref[h], (ATT_QBLK, ATT_RVEC))
            toe = pltpu.roll(src, ATT_RVEC - ATT_QBLK, 1, stride=1, stride_axis=0)[:, :ATT_KBLK]
            for v in range(ATT_VARIANTS):
                ok = band & (j >= (ATT_KBLK - ATT_QBLK) - ATT_QBLK * v)
                b = jnp.where(ok, toe, NEG)
                for half in range(2):
                    bias_sc[v, h, half] = b[half * ATT_HALF:(half + 1) * ATT_HALF,
                                            half * ATT_HALF:half * ATT_HALF + ATT_WIN]

    v = jnp.minimum(i, ATT_VARIANTS - 1)
    units = [(h, half) for h in range(hb) for half in range(2)]

    def lanes(h):
        return slice(h * ATT_HEADDIM, (h + 1) * ATT_HEADDIM)

    def rows(half):
        return slice(half * ATT_HALF, (half + 1) * ATT_HALF)

    def window(refs, h, half):
        cat = jnp.concatenate([r[0, :, lanes(h)] for r in refs], axis=0)
        return cat[half * ATT_HALF:half * ATT_HALF + ATT_WIN]

    def scores(h, half):
        s = lax.dot_general(q_ref[0, rows(half), lanes(h)], window((k0_ref, k1_ref, k2_ref), h, half),
                            (((1,), (1,)), ((), ())), preferred_element_type=F32)
        return s + bias_sc[v, h, half]

    def probs(s):
        p = jnp.exp2(s - jnp.max(s, axis=1, keepdims=True))
        return p.astype(BF16), jnp.sum(p, axis=1, keepdims=True)

    def finish(h, half, p, den):
        o = jnp.dot(p, window((v0_ref, v1_ref, v2_ref), h, half), preferred_element_type=F32) / den
        o_ref[0, rows(half), lanes(h)] = (o * _silu(z_ref[0, rows(half), lanes(h)].astype(F32))).astype(o_ref.dtype)

    stages = [lambda u, _: scores(*units[u]), lambda u, s: probs(s), lambda u, pd: finish(*units[u], *pd)]
    inflight = {}
    n = len(units)
    for step in range(n + len(stages) - 1):
        for k, stage in enumerate(stages):
            u = step - k
            if 0 <= u < n:
                inflight[u] = stage(u, inflight.get(u))


def _rel_source_rows(rel_table):
    u = np.arange(ATT_RVEC)
    bucket = np.clip(ATT_KBLK - u, -REL_MAX, REL_MAX) + REL_MAX
    return (rel_table[:, bucket].astype(F32) * LOG2E)[:, None, :]


def _attention(qz, kv, rel_rows, hb=8):
    bsz, s, two_d = qz.shape
    d = two_d // 2
    hw = hb * ATT_HEADDIM
    ng = d // hw
    cur = lambda off: pl.BlockSpec((1, ATT_QBLK, hw), lambda hg, b, i: (b, i, off + hg))
    back = lambda off, n: pl.BlockSpec((1, ATT_QBLK, hw), lambda hg, b, i: (b, jnp.maximum(i - n, 0), off + hg))
    return pl.pallas_call(
        functools.partial(_attn_kernel, hb=hb),
        grid=(ng, bsz, s // ATT_QBLK),
        in_specs=[cur(0), cur(ng), back(0, 2), back(0, 1), cur(0), back(ng, 2), back(ng, 1), cur(ng),
                  pl.BlockSpec((hb, 1, ATT_RVEC), lambda hg, b, i: (hg, 0, 0))],
        out_specs=pl.BlockSpec((1, ATT_QBLK, hw), lambda hg, b, i: (b, i, hg)),
        out_shape=jax.ShapeDtypeStruct((bsz, s, d), BF16),
        scratch_shapes=[pltpu.VMEM((ATT_VARIANTS, hb, 2, ATT_HALF, ATT_WIN), F32)],
        compiler_params=_params(3), name="band_attention",
    )(qz, qz, kv, kv, kv, kv, kv, kv, rel_rows)


def kernel(x, c, ada_w, ada_b, pre_norm_g, post_norm_g, ssm_w_in, ssm_conv_w, ssm_conv_b, ssm_dt_bias, ssm_a_log, ssm_d, ssm_norm_g, ssm_w_out, kv_norm_g, w_kv, att_w_in, att_rel_bias, att_w_out):
    bsz, s, d = x.shape
    depth = ada_w.shape[0]
    n_a = ssm_w_in.shape[0]
    t = bsz * s
    d_inner = ssm_w_out.shape[1]
    nheads = ssm_dt_bias.shape[1]
    zx_cols = ssm_w_in.shape[2] - nheads

    mod = _ada_modulation(c, ada_w, ada_b)
    shift = [mod[l, :, None, 0:d] for l in range(depth)]
    scale = [mod[l, :, None, d:2 * d] for l in range(depth)]
    gate = [mod[l, :, None, 2 * d:3 * d] for l in range(depth)]

    att_d = att_w_out.shape[1]
    qscale = jnp.concatenate([jnp.full((att_d,), ATT_HEADDIM ** -0.5 * LOG2E, F32), jnp.ones((att_d,), F32)])
    n_b = depth - n_a
    w_in_b = ssm_w_in[0:1].astype(BF16)
    w_out_b = w_kv_b = None
    att_in_b, att_out_b = [None] * n_b, [None] * n_b

    (h,) = _resnorm(x, pre=(shift[0], scale[0], pre_norm_g[0]))
    kv = None
    for layer in range(depth):
        h2 = h.reshape(t, d)
        if layer < n_a:
            i = layer
            jobs = [_CastJob(ssm_w_out, i, None)]
            if i + 1 < n_a:
                jobs.append(_CastJob(ssm_w_in, i + 1, None))
            else:
                jobs += [_CastJob(w_kv[None], 0, None), _CastJob(att_w_in, 0, qscale)]
            zx, w_out_b, *rest = _ssm_in_proj(h2, w_in_b, 0, ssm_conv_w[i], ssm_conv_b[i], d_inner, s, jobs=jobs)
            dt = _matmul(h2, w_in_b, 0, F32, 1024, LANES, "ssm_dt_proj", col0=zx_cols, ncols=nheads,
                         bias=ssm_dt_bias[i])
            if i + 1 < n_a:
                w_in_b = rest[0][None]
            else:
                w_kv_b, att_in_b[0] = rest
            yn = _ssd_scan(zx.reshape(bsz, s, zx_cols), dt.reshape(bsz, s, nheads), ssm_a_log[i], ssm_d[i],
                           ssm_norm_g[i])
            y = _matmul(yn.reshape(t, d_inner), w_out_b[None], 0, BF16, 1024, 512, "ssm_out_proj")
        else:
            i = layer - n_a
            qz = _matmul(h2, att_in_b[i][None], 0, BF16, 1024, 1024, "att_in_proj")
            o = _attention(qz.reshape(bsz, s, 2 * att_d), kv, _rel_source_rows(att_rel_bias[i]))
            y = _matmul(o.reshape(t, att_d), att_out_b[i][None], 0, BF16, 1024, 1024, "att_out_proj")
        post = (y.reshape(bsz, s, d), gate[layer], post_norm_g[layer])
        if layer + 1 < depth:
            nxt = layer + 1
            pre = (shift[nxt], scale[nxt], pre_norm_g[nxt])
            if nxt == n_a:
                x, h, hkv = _resnorm(x, post=post, pre=pre, kv_g=kv_norm_g)
                jobs = [_CastJob(att_w_out, 0, None)]
                for bi in range(1, n_b):
                    jobs += [_CastJob(att_w_in, bi, qscale), _CastJob(att_w_out, bi, None)]
                kv, att_out_b[0], *rest = _matmul(hkv.reshape(t, d), w_kv_b[None], 0, BF16, 1024, 1024, "kv_proj",
                                                  jobs=jobs)
                att_in_b[1:], att_out_b[1:] = rest[0::2], rest[1::2]
                kv = kv.reshape(bsz, s, w_kv.shape[1])
            else:
                x, h = _resnorm(x, post=post, pre=pre)
        else:
            (x,) = _resnorm(x, post=post)
    return x
```

```python
import functools
from typing import NamedTuple, Optional

import jax
import jax.numpy as jnp
import numpy as np
from jax import lax
from jax.experimental import pallas as pl
from jax.experimental.pallas import tpu as pltpu

F32 = jnp.float32
BF16 = jnp.bfloat16

EPS = 1e-6
CHUNK = 64
LANES = 128
MXU_DIM = 256
V7X_VMEM_BYTES = 64 * 1024 * 1024
VMEM_LIMIT = V7X_VMEM_BYTES - 8 * 1024 * 1024

SSM_HEADDIM = 64
SSM_GROUPS = 8
SSM_STATE = 128
SSM_CONV = 4
ATT_HEADDIM = 128
ATT_LEFT_CHUNKS = 8
REL_MAX = 128
ATT_QBLK = 4 * CHUNK
ATT_KBLK = 3 * ATT_QBLK
ATT_HALF = ATT_QBLK // 2
ATT_WIN = ATT_KBLK - ATT_HALF
ATT_RVEC = 1024
ATT_VARIANTS = 3
LOG2E = 1.4426950408889634
NEG = -1e30
SIDE_JOB_BLOCKS = 256


def _params(n_grid, vmem=VMEM_LIMIT):
    return pltpu.CompilerParams(dimension_semantics=("arbitrary",) * n_grid, vmem_limit_bytes=vmem)


def _silu_of_twice(h):
    return h + h * jnp.tanh(h)


def _silu(v):
    return _silu_of_twice(0.5 * v)


def _softplus(v):
    return jnp.maximum(v, 0.0) + jnp.log1p(jnp.exp(-jnp.abs(v)))


def _ada_kernel(ct_ref, w_ref, b_ref, o_ref, sb_ref, *, nb, tn):
    d = ct_ref.shape[0]

    @pl.when((pl.program_id(0) == 0) & (pl.program_id(1) == 0))
    def _():
        s = _silu(ct_ref[...])
        for b in range(nb):
            sb_ref[b] = jnp.broadcast_to(s[:, b:b + 1], (d, LANES))

    rows = []
    for b in range(nb):
        cols = []
        for t in range(tn // LANES):
            w = w_ref[0, :, t * LANES:(t + 1) * LANES]
            cols.append(jnp.sum(w * sb_ref[b], axis=0, keepdims=True))
        rows.append(jnp.concatenate(cols, axis=1))
    o_ref[0] = jnp.concatenate(rows, axis=0) + b_ref[0]


def _ada_modulation(c, ada_w, ada_b, tn=512):
    depth, d, n3 = ada_w.shape
    nb = c.shape[0]
    return pl.pallas_call(
        functools.partial(_ada_kernel, nb=nb, tn=tn),
        grid=(depth, n3 // tn),
        in_specs=[pl.BlockSpec((d, nb), lambda l, j: (0, 0)),
                  pl.BlockSpec((1, d, tn), lambda l, j: (l, 0, j)),
                  pl.BlockSpec((1, 1, tn), lambda l, j: (l, 0, j))],
        out_specs=pl.BlockSpec((1, nb, tn), lambda l, j: (l, 0, j)),
        out_shape=jax.ShapeDtypeStruct((depth, nb, n3), F32),
        scratch_shapes=[pltpu.VMEM((nb, d, LANES), F32)],
        compiler_params=_params(2),
        name="ada_modulation",
    )(c.T, ada_w, ada_b.reshape(depth, 1, n3))


def _rms(v, g):
    var = jnp.mean(v * v, axis=-1, keepdims=True)
    return v * lax.rsqrt(var + EPS) * g


def _resnorm_kernel(*refs, has_post, has_pre, has_kv):
    it = iter(refs)
    x_ref = next(it)
    if has_post:
        y_ref, gate_ref, gpost_ref = next(it), next(it), next(it)
    if has_pre:
        shift_ref, scale_ref, gpre_ref = next(it), next(it), next(it)
    if has_kv:
        gkv_ref = next(it)
    x = x_ref[0]
    if has_post:
        x = x + gate_ref[0] * _rms(y_ref[0].astype(F32), gpost_ref[...])
        next(it)[0] = x
    if has_pre:
        h = _rms(x, gpre_ref[...]) * (1.0 + scale_ref[0]) + shift_ref[0]
        next(it)[0] = h.astype(BF16)
    if has_kv:
        next(it)[0] = _rms(x, gkv_ref[...]).astype(BF16)


def _resnorm(x, post=None, pre=None, kv_g=None, tr=256):
    bsz, s, d = x.shape
    row = pl.BlockSpec((1, tr, d), lambda b, i: (b, i, 0))
    per_b = pl.BlockSpec((1, 1, d), lambda b, i: (b, 0, 0))
    vec = pl.BlockSpec((1, d), lambda b, i: (0, 0))
    args, in_specs, out_specs, out_shape = [x], [row], [], []
    if post is not None:
        y, gate, g_post = post
        args += [y, gate, g_post.reshape(1, d)]
        in_specs += [row, per_b, vec]
        out_specs.append(row)
        out_shape.append(jax.ShapeDtypeStruct((bsz, s, d), F32))
    if pre is not None:
        shift, scale, g_pre = pre
        args += [shift, scale, g_pre.reshape(1, d)]
        in_specs += [per_b, per_b, vec]
        out_specs.append(row)
        out_shape.append(jax.ShapeDtypeStruct((bsz, s, d), BF16))
    if kv_g is not None:
        args.append(kv_g.reshape(1, d))
        in_specs.append(vec)
        out_specs.append(row)
        out_shape.append(jax.ShapeDtypeStruct((bsz, s, d), BF16))
    return pl.pallas_call(
        functools.partial(_resnorm_kernel, has_post=post is not None, has_pre=pre is not None,
                          has_kv=kv_g is not None),
        grid=(bsz, s // tr), in_specs=in_specs, out_specs=out_specs, out_shape=out_shape,
        compiler_params=_params(2), name="resnorm",
    )(*args)


class _CastJob(NamedTuple):
    w: jax.Array
    layer: int
    scale: Optional[jax.Array]


def _job_specs(jobs, ni, nj):
    nblk = min(SIDE_JOB_BLOCKS, 1 << ((ni * nj).bit_length() - 1))
    args, in_specs, out_specs, out_shape = [], [], [], []
    for job in jobs:
        _, r, c = job.w.shape
        rb = r // nblk
        blk = lambda i, j: jnp.minimum(i * nj + j, nblk - 1)
        args.append(job.w)
        in_specs.append(pl.BlockSpec((None, rb, c), lambda i, j, l=job.layer: (l, blk(i, j), 0)))
        if job.scale is not None:
            args.append(job.scale.reshape(1, c))
            in_specs.append(pl.BlockSpec((1, c), lambda i, j: (0, 0)))
        out_specs.append(pl.BlockSpec((rb, c), lambda i, j: (blk(i, j), 0)))
        out_shape.append(jax.ShapeDtypeStruct((r, c), BF16))
    return args, in_specs, out_specs, out_shape


def _run_cast_jobs(in_refs, out_refs, scaled):
    it = iter(in_refs)
    for o_ref, has_scale in zip(out_refs, scaled):
        w = next(it)[...]
        if has_scale:
            w = w * next(it)[...]
        o_ref[...] = w.astype(o_ref.dtype)


def _n_job_inputs(scaled):
    return len(scaled) + sum(scaled)


def _mm_kernel(*refs, has_bias, scaled):
    n_in = 2 + has_bias + _n_job_inputs(scaled)
    a_ref, b_ref = refs[:2]
    o_ref = refs[n_in]
    _run_cast_jobs(refs[2 + has_bias:n_in], refs[n_in + 1:], scaled)
    tn = o_ref.shape[1]
    sub = min(MXU_DIM, tn)
    for c in range(tn // sub):
        cs = slice(c * sub, (c + 1) * sub)
        acc = jnp.dot(a_ref[...], b_ref[:, cs], preferred_element_type=F32)
        if has_bias:
            acc = _softplus(acc + refs[2][:, cs])
        o_ref[:, cs] = acc.astype(o_ref.dtype)


def _matmul(a, b, layer, out_dtype, tm, tn, name, col0=0, ncols=None, bias=None, jobs=()):
    m, k = a.shape
    n = b.shape[2] - col0 if ncols is None else ncols
    tm, tn = min(tm, m), min(tn, n)
    jb0 = col0 // tn
    args = [a, b]
    in_specs = [pl.BlockSpec((tm, k), lambda i, j: (i, 0)),
                pl.BlockSpec((None, k, tn), lambda i, j: (layer, 0, jb0 + j))]
    if bias is not None:
        args.append(bias.reshape(1, n))
        in_specs.append(pl.BlockSpec((1, tn), lambda i, j: (0, j)))
    j_args, j_in, j_out, j_shape = _job_specs(jobs, m // tm, n // tn)
    out = pl.pallas_call(
        functools.partial(_mm_kernel, has_bias=bias is not None, scaled=tuple(jb.scale is not None for jb in jobs)),
        grid=(m // tm, n // tn),
        in_specs=in_specs + j_in,
        out_specs=[pl.BlockSpec((tm, tn), lambda i, j: (i, j))] + j_out,
        out_shape=[jax.ShapeDtypeStruct((m, n), out_dtype)] + j_shape,
        compiler_params=_params(2), name=name,
    )(*args, *j_args)
    return out if jobs else out[0]


CONV_HALO = 8


def _ssm_in_kernel(*refs, n_plain, tiles_per_seq, scaled):
    n_in = 4 + _n_job_inputs(scaled)
    a_ref, w_ref, cw_ref, cb_ref = refs[:4]
    o_ref, tail_ref = refs[n_in], refs[-1]
    _run_cast_jobs(refs[4:n_in], refs[n_in + 1:-1], scaled)
    i, j = pl.program_id(0), pl.program_id(1)
    tm, tn = o_ref.shape
    sub = min(MXU_DIM, tn)

    @pl.when((i == 0) & (j == 0))
    def _():
        tail_ref[...] = jnp.zeros_like(tail_ref)

    def cols(c):
        return slice(c * sub, (c + 1) * sub)

    def sub_dots(epilogue):
        for c in range(tn // sub):
            epilogue(jnp.dot(a_ref[...], w_ref[:, cols(c)], preferred_element_type=F32), cols(c))

    @pl.when(j < n_plain)
    def _():
        def plain(acc, cs):
            o_ref[:, cs] = acc.astype(o_ref.dtype)
        sub_dots(plain)

    @pl.when(j >= n_plain)
    def _():
        jc = j - n_plain
        seq_start = i % tiles_per_seq == 0
        nt = tm // CONV_HALO
        row = lax.broadcasted_iota(jnp.int32, (nt, CONV_HALO, sub), 1)

        def delayed(v, before, d):
            r = pltpu.roll(v, d, 1)
            prev = jnp.concatenate([pltpu.roll(before, d, 0)[None], r[:-1]], axis=0)
            return jnp.where(row < d, prev, r)

        def conv(acc, cs):
            tail = jnp.where(seq_start, 0.0, tail_ref[jc, :, cs])
            tail_ref[jc, :, cs] = acc[tm - CONV_HALO:]
            w = cw_ref[:, cs]
            w0, w1, w2, w3 = (w[k:k + 1] for k in range(SSM_CONV))
            a = acc.reshape(nt, CONV_HALO, sub)
            a1 = delayed(a, tail, 1)
            pair = w1 * a + w0 * a1
            pair_tail = w1 * tail + w0 * pltpu.roll(tail, 1, 0)
            half = cb_ref[:, cs] + w3 * a + w2 * a1 + delayed(pair, pair_tail, 2)
            o_ref[:, cs] = (half + half * jnp.tanh(half)).reshape(tm, sub).astype(o_ref.dtype)
        sub_dots(conv)


def _ssm_in_proj(h2, w_in, layer, conv_w, conv_b, n_plain_cols, seq, jobs=(), tm=1024, tn=1024):
    t, k = h2.shape
    c = conv_w.shape[1]
    n = n_plain_cols + c
    n_plain = n_plain_cols // tn
    conv_col = lambda i, j: (0, jnp.maximum(j - n_plain, 0))
    j_args, j_in, j_out, j_shape = _job_specs(jobs, t // tm, n // tn)
    return pl.pallas_call(
        functools.partial(_ssm_in_kernel, n_plain=n_plain, tiles_per_seq=seq // tm,
                          scaled=tuple(jb.scale is not None for jb in jobs)),
        grid=(t // tm, n // tn),
        in_specs=[pl.BlockSpec((tm, k), lambda i, j: (i, 0)),
                  pl.BlockSpec((None, k, tn), lambda i, j: (layer, 0, j)),
                  pl.BlockSpec((SSM_CONV, tn), conv_col),
                  pl.BlockSpec((1, tn), conv_col)] + j_in,
        out_specs=[pl.BlockSpec((tm, tn), lambda i, j: (i, j))] + j_out,
        out_shape=[jax.ShapeDtypeStruct((t, n), BF16)] + j_shape,
        scratch_shapes=[pltpu.VMEM((c // tn, CONV_HALO, tn), F32)],
        compiler_params=_params(2), name="ssm_in_proj",
    )(h2, w_in, 0.5 * conv_w, 0.5 * conv_b.reshape(1, c), *j_args)


def _scan_kernel(x_ref, b_ref, c_ref, z_ref, dta_ref, dtt_ref, alr_ref, alc_ref,
                 dexp_ref, ng_ref, ex_ref, o_ref, st_ref, *, nsub):
    L = CHUNK
    P = SSM_HEADDIM
    hp = LANES // P
    nheads = alr_ref.shape[2]
    npair = nheads // hp
    qh = MXU_DIM // P
    nquad = nheads // qh

    @pl.when(pl.program_id(2) == 0)
    def _():
        st_ref[...] = jnp.zeros_like(st_ref)

    a_r = -jnp.exp(alr_ref[0])
    a_c = -jnp.exp(alc_ref[0])
    lane = lax.broadcasted_iota(jnp.int32, (L, LANES), 1)
    sub = lax.broadcasted_iota(jnp.int32, (L, LANES), 0)
    lo = lane < P
    s_of_lane = lane & (P - 1)
    causal2 = s_of_lane <= sub
    tri = (lax.broadcasted_iota(jnp.int32, (L, L), 1) <= lax.broadcasted_iota(jnp.int32, (L, L), 0)).astype(BF16)
    tri_t2 = (sub <= s_of_lane).astype(BF16)
    blk = lax.broadcasted_iota(jnp.int32, (L, qh * P), 1) // P
    dexp = dexp_ref[...]
    ng = ng_ref[...]

    def pieces(v, n=3):
        out, rem = [], v
        for _ in range(n):
            hi = rem.astype(BF16).astype(F32)
            out.append(hi)
            rem = rem - hi
        return out

    def local(ci):
        rows = slice(ci * L, (ci + 1) * L)
        x = x_ref[0, rows, :]
        bm = b_ref[0, rows, :]
        cm = c_ref[0, rows, :]
        dt = dta_ref[0, ci, 0]
        dt_t = dtt_ref[0, ci, 0]
        acum = sum(jnp.dot(tri, p.astype(BF16), preferred_element_type=F32) for p in pieces(dt * a_r))
        acum_t = sum(jnp.dot(p.astype(BF16), tri_t2, preferred_element_type=F32)
                     for p in pieces(dt_t[:, :L] * a_c))
        e_out = jnp.exp(acum[L - 1:L, :] - acum) * dt
        ex = ex_ref[...]
        spr = lambda v, n: jnp.dot(jnp.concatenate(pieces(v, n), axis=1).astype(BF16), ex[:n * nheads],
                                   preferred_element_type=F32)
        col, sc_out = spr(acum, 3), spr(e_out, 2)
        sc_in = jnp.exp(col)
        cb2 = lax.dot_general(cm, jnp.concatenate([bm, bm], axis=0), (((1,), (1,)), ((), ())),
                              preferred_element_type=F32)
        xf = x.astype(F32)

        w_tiles = []
        for j in range(npair):
            r0, r1 = hp * j, hp * j + 1
            row = jnp.where(lo[0:1], acum_t[r0:r0 + 1], acum_t[r1:r1 + 1])
            dtr = jnp.where(lo[0:1], dt_t[r0:r0 + 1], dt_t[r1:r1 + 1])
            seg = jnp.where(causal2, col[:, j * LANES:(j + 1) * LANES] - row, NEG)
            w_tiles.append((jnp.exp(seg) * (cb2 * dtr)).astype(BF16))

        y_parts = []
        for q in range(nquad):
            w4 = jnp.concatenate(w_tiles[q * qh // hp:(q + 1) * qh // hp], axis=1)
            x4 = x[:, q * qh * P:(q + 1) * qh * P]
            xbd = jnp.concatenate([jnp.where(blk == i, x4, jnp.zeros_like(x4)) for i in range(qh)], axis=0)
            y_parts.append(jnp.dot(w4, xbd, preferred_element_type=F32))
        y_loc = jnp.concatenate(y_parts, axis=1) + xf * dexp

        xw = x * sc_out.astype(BF16)
        upd = lax.dot_general(bm, xw, (((0,), (0,)), ((), ())), preferred_element_type=F32)
        return cm, sc_in, y_loc, upd

    def carried(ci, st, cm, sc_in, y_loc, upd):
        rows = slice(ci * L, (ci + 1) * L)
        y = y_loc + jnp.dot(cm, st.astype(BF16), preferred_element_type=F32) * sc_in
        yg = y * _silu_of_twice(z_ref[0, rows, :].astype(F32))
        o_ref[0, rows, :] = _rms(yg, ng).astype(o_ref.dtype)
        return st * sc_in[L - 1:L, :] + upd

    st = st_ref[...]
    for ci in range(nsub):
        st = carried(ci, st, *local(ci))
    st_ref[...] = st


def _spread_matrix(nheads_g, parts=3):
    rows = np.arange(parts * nheads_g)
    cols = np.arange(nheads_g * SSM_HEADDIM)
    return jnp.asarray((rows % nheads_g)[:, None] == (cols // SSM_HEADDIM)[None, :], dtype=BF16)


def _ssd_scan(zx, dt, a_log, d_skip, norm_g, nsub=16):
    bsz, s, _ = zx.shape
    g = SSM_GROUPS
    nheads = dt.shape[-1]
    r = nheads // g
    gw = r * SSM_HEADDIM
    d_inner = nheads * SSM_HEADDIM
    nc = s // CHUNK
    lb = nsub * CHUNK
    dt5 = dt.reshape(bsz, nc, CHUNK, g, r)
    dt_a = dt5.transpose(0, 1, 3, 2, 4)
    dt_t = dt5.transpose(0, 1, 3, 4, 2)
    dt_t = jnp.concatenate([dt_t, dt_t], axis=-1)
    x_off = d_inner // gw
    b_off = 2 * d_inner // SSM_STATE
    spread = _spread_matrix(r)
    row_spec = pl.BlockSpec((1, 1, r), lambda b, gi, ci: (gi, 0, 0))
    col_spec = pl.BlockSpec((1, r, 1), lambda b, gi, ci: (gi, 0, 0))
    wide = lambda off: pl.BlockSpec((1, lb, gw), lambda b, gi, ci: (b, ci, off + gi))
    narrow = lambda off: pl.BlockSpec((1, lb, SSM_STATE), lambda b, gi, ci: (b, ci, off + gi))
    return pl.pallas_call(
        functools.partial(_scan_kernel, nsub=nsub),
        grid=(bsz, g, nc // nsub),
        in_specs=[wide(x_off), narrow(b_off), narrow(b_off + g), wide(0),
                  pl.BlockSpec((1, nsub, 1, CHUNK, r), lambda b, gi, ci: (b, ci, gi, 0, 0)),
                  pl.BlockSpec((1, nsub, 1, r, 2 * CHUNK), lambda b, gi, ci: (b, ci, gi, 0, 0)),
                  row_spec, col_spec,
                  pl.BlockSpec((1, gw), lambda b, gi, ci: (0, gi)),
                  pl.BlockSpec((1, gw), lambda b, gi, ci: (0, gi)),
                  pl.BlockSpec(spread.shape, lambda b, gi, ci: (0, 0))],
        out_specs=wide(0),
        out_shape=jax.ShapeDtypeStruct((bsz, s, d_inner), BF16),
        scratch_shapes=[pltpu.VMEM((SSM_STATE, gw), F32)],
        compiler_params=_params(3), name="ssd_scan",
    )(zx, zx, zx, zx, dt_a, dt_t, a_log.reshape(g, 1, r), a_log.reshape(g, r, 1),
      jnp.repeat(d_skip, SSM_HEADDIM).reshape(1, d_inner), norm_g.reshape(1, d_inner), spread)


def _attn_kernel(q_ref, z_ref, k0_ref, k1_ref, k2_ref, v0_ref, v1_ref, v2_ref, rv_ref, o_ref, bias_sc, *, hb):
    i = pl.program_id(2)

    @pl.when((pl.program_id(1) == 0) & (i == 0))
    def _():
        l = lax.broadcasted_iota(jnp.int32, (ATT_QBLK, ATT_KBLK), 0)
        j = lax.broadcasted_iota(jnp.int32, (ATT_QBLK, ATT_KBLK), 1)
        lc, jc = l // CHUNK, j // CHUNK
        band = (jc >= lc) & (jc <= lc + ATT_LEFT_CHUNKS)
        for h in range(hb):
            src = jnp.broadcast_to(rv_ref[h], (ATT_QBLK, ATT_RVEC))
            toe = pltpu.roll(src, ATT_RVEC - ATT_QBLK, 1, stride=1, stride_axis=0)[:, :ATT_KBLK]
            for v in range(ATT_VARIANTS):
                ok = band & (j >= (ATT_KBLK - ATT_QBLK) - ATT_QBLK * v)
                b = jnp.where(ok, toe, NEG)
                for half in range(2):
                    bias_sc[v, h, half] = b[half * ATT_HALF:(half + 1) * ATT_HALF,
                                            half * ATT_HALF:half * ATT_HALF + ATT_WIN]

    v = jnp.minimum(i, ATT_VARIANTS - 1)
    units = [(h, half) for h in range(hb) for half in range(2)]

    def lanes(h):
        return slice(h * ATT_HEADDIM, (h + 1) * ATT_HEADDIM)

    def rows(half):
        return slice(half * ATT_HALF, (half + 1) * ATT_HALF)

    def window(refs, h, half):
        cat = jnp.concatenate([r[0, :, lanes(h)] for r in refs], axis=0)
        return cat[half * ATT_HALF:half * ATT_HALF + ATT_WIN]

    def scores(h, half):
        s = lax.dot_general(q_ref[0, rows(half), lanes(h)], window((k0_ref, k1_ref, k2_ref), h, half),
                            (((1,), (1,)), ((), ())), preferred_element_type=F32)
        return s + bias_sc[v, h, half]

    def probs(s):
        p = jnp.exp2(s - jnp.max(s, axis=1, keepdims=True))
        return p.astype(BF16), jnp.sum(p, axis=1, keepdims=True)

    def finish(h, half, p, den):
        o = jnp.dot(p, window((v0_ref, v1_ref, v2_ref), h, half), preferred_element_type=F32) / den
        gate = _silu_of_twice(z_ref[0, rows(half), lanes(h)].astype(F32))
        o_ref[0, rows(half), lanes(h)] = (o * gate).astype(o_ref.dtype)

    stages = [lambda u, _: scores(*units[u]), lambda u, s: probs(s), lambda u, pd: finish(*units[u], *pd)]
    inflight = {}
    n = len(units)
    for step in range(n + len(stages) - 1):
        for k, stage in enumerate(stages):
            u = step - k
            if 0 <= u < n:
                inflight[u] = stage(u, inflight.get(u))


def _rel_source_rows(rel_table):
    u = np.arange(ATT_RVEC)
    bucket = np.clip(ATT_KBLK - u, -REL_MAX, REL_MAX) + REL_MAX
    return (rel_table[:, bucket].astype(F32) * LOG2E)[:, None, :]


def _attention(qz, kv, rel_rows, hb=8):
    bsz, s, two_d = qz.shape
    d = two_d // 2
    hw = hb * ATT_HEADDIM
    ng = d // hw
    cur = lambda off: pl.BlockSpec((1, ATT_QBLK, hw), lambda hg, b, i: (b, i, off + hg))
    back = lambda off, n: pl.BlockSpec((1, ATT_QBLK, hw), lambda hg, b, i: (b, jnp.maximum(i - n, 0), off + hg))
    return pl.pallas_call(
        functools.partial(_attn_kernel, hb=hb),
        grid=(ng, bsz, s // ATT_QBLK),
        in_specs=[cur(0), cur(ng), back(0, 2), back(0, 1), cur(0), back(ng, 2), back(ng, 1), cur(ng),
                  pl.BlockSpec((hb, 1, ATT_RVEC), lambda hg, b, i: (hg, 0, 0))],
        out_specs=pl.BlockSpec((1, ATT_QBLK, hw), lambda hg, b, i: (b, i, hg)),
        out_shape=jax.ShapeDtypeStruct((bsz, s, d), BF16),
        scratch_shapes=[pltpu.VMEM((ATT_VARIANTS, hb, 2, ATT_HALF, ATT_WIN), F32)],
        compiler_params=_params(3), name="band_attention",
    )(qz, qz, kv, kv, kv, kv, kv, kv, rel_rows)


def kernel(x, c, ada_w, ada_b, pre_norm_g, post_norm_g, ssm_w_in, ssm_conv_w, ssm_conv_b, ssm_dt_bias, ssm_a_log, ssm_d, ssm_norm_g, ssm_w_out, kv_norm_g, w_kv, att_w_in, att_rel_bias, att_w_out):
    bsz, s, d = x.shape
    depth = ada_w.shape[0]
    n_a = ssm_w_in.shape[0]
    t = bsz * s
    d_inner = ssm_w_out.shape[1]
    nheads = ssm_dt_bias.shape[1]
    zx_cols = ssm_w_in.shape[2] - nheads

    mod = _ada_modulation(c, ada_w, ada_b)
    shift = [mod[l, :, None, 0:d] for l in range(depth)]
    scale = [mod[l, :, None, d:2 * d] for l in range(depth)]
    gate = [mod[l, :, None, 2 * d:3 * d] for l in range(depth)]

    att_d = att_w_out.shape[1]
    qscale = jnp.concatenate([jnp.full((att_d,), ATT_HEADDIM ** -0.5 * LOG2E, F32), jnp.full((att_d,), 0.5, F32)])
    zscale = jnp.concatenate([jnp.full((d_inner,), 0.5, F32), jnp.ones((ssm_w_in.shape[2] - d_inner,), F32)])
    n_b = depth - n_a
    w_in_b = (ssm_w_in[0:1] * zscale).astype(BF16)
    w_out_b = w_kv_b = None
    att_in_b, att_out_b = [None] * n_b, [None] * n_b

    (h,) = _resnorm(x, pre=(shift[0], scale[0], pre_norm_g[0]))
    kv = None
    for layer in range(depth):
        h2 = h.reshape(t, d)
        if layer < n_a:
            i = layer
            jobs = [_CastJob(ssm_w_out, i, None)]
            if i + 1 < n_a:
                jobs.append(_CastJob(ssm_w_in, i + 1, zscale))
            else:
                jobs += [_CastJob(w_kv[None], 0, None), _CastJob(att_w_in, 0, qscale)]
            zx, w_out_b, *rest = _ssm_in_proj(h2, w_in_b, 0, ssm_conv_w[i], ssm_conv_b[i], d_inner, s, jobs=jobs)
            dt = _matmul(h2, w_in_b, 0, F32, 1024, LANES, "ssm_dt_proj", col0=zx_cols, ncols=nheads,
                         bias=ssm_dt_bias[i])
            if i + 1 < n_a:
                w_in_b = rest[0][None]
            else:
                w_kv_b, att_in_b[0] = rest
            yn = _ssd_scan(zx.reshape(bsz, s, zx_cols), dt.reshape(bsz, s, nheads), ssm_a_log[i], ssm_d[i],
                           ssm_norm_g[i])
            y = _matmul(yn.reshape(t, d_inner), w_out_b[None], 0, BF16, 1024, 512, "ssm_out_proj")
        else:
            i = layer - n_a
            qz = _matmul(h2, att_in_b[i][None], 0, BF16, 1024, 1024, "att_in_proj")
            o = _attention(qz.reshape(bsz, s, 2 * att_d), kv, _rel_source_rows(att_rel_bias[i]))
            y = _matmul(o.reshape(t, att_d), att_out_b[i][None], 0, BF16, 1024, 1024, "att_out_proj")
        post = (y.reshape(bsz, s, d), gate[layer], post_norm_g[layer])
        if layer + 1 < depth:
            nxt = layer + 1
            pre = (shift[nxt], scale[nxt], pre_norm_g[nxt])
            if nxt == n_a:
                x, h, hkv = _resnorm(x, post=post, pre=pre, kv_g=kv_norm_g)
                jobs = [_CastJob(att_w_out, 0, None)]
                for bi in range(1, n_b):
                    jobs += [_CastJob(att_w_in, bi, qscale), _CastJob(att_w_out, bi, None)]
                kv, att_out_b[0], *rest = _matmul(hkv.reshape(t, d), w_kv_b[None], 0, BF16, 1024, 1024, "kv_proj",
                                                  jobs=jobs)
                att_in_b[1:], att_out_b[1:] = rest[0::2], rest[1::2]
                kv = kv.reshape(bsz, s, w_kv.shape[1])
            else:
                x, h = _resnorm(x, post=post, pre=pre)
        else:
            (x,) = _resnorm(x, post=post)
    return x
```

```python
import functools
from typing import NamedTuple, Optional

import jax
import jax.numpy as jnp
import numpy as np
from jax import lax
from jax.experimental import pallas as pl
from jax.experimental.pallas import tpu as pltpu

F32 = jnp.float32
BF16 = jnp.bfloat16

EPS = 1e-6
CHUNK = 64
LANES = 128
MXU_DIM = 256
V7X_VMEM_BYTES = 64 * 1024 * 1024
VMEM_LIMIT = V7X_VMEM_BYTES - 8 * 1024 * 1024

SSM_HEADDIM = 64
SSM_GROUPS = 8
SSM_STATE = 128
SSM_CONV = 4
ATT_HEADDIM = 128
ATT_LEFT_CHUNKS = 8
REL_MAX = 128
ATT_QBLK = 4 * CHUNK
ATT_KBLK = 3 * ATT_QBLK
ATT_HALF = ATT_QBLK // 2
ATT_WIN = ATT_KBLK - ATT_HALF
ATT_RVEC = 1024
ATT_VARIANTS = 3
LOG2E = 1.4426950408889634
NEG = -1e30
SIDE_JOB_BLOCKS = 256


def _params(n_grid, vmem=VMEM_LIMIT):
    return pltpu.CompilerParams(dimension_semantics=("arbitrary",) * n_grid, vmem_limit_bytes=vmem)


def _silu_of_twice(h):
    return h + h * jnp.tanh(h)


def _silu(v):
    return _silu_of_twice(0.5 * v)


def _softplus(v):
    return jnp.maximum(v, 0.0) + jnp.log1p(jnp.exp(-jnp.abs(v)))


def _ada_kernel(ct_ref, w_ref, b_ref, o_ref, sb_ref, *, nb, tn):
    d = ct_ref.shape[0]

    @pl.when((pl.program_id(0) == 0) & (pl.program_id(1) == 0))
    def _():
        s = _silu(ct_ref[...])
        for b in range(nb):
            sb_ref[b] = jnp.broadcast_to(s[:, b:b + 1], (d, LANES))

    rows = []
    for b in range(nb):
        cols = []
        for t in range(tn // LANES):
            w = w_ref[0, :, t * LANES:(t + 1) * LANES]
            cols.append(jnp.sum(w * sb_ref[b], axis=0, keepdims=True))
        rows.append(jnp.concatenate(cols, axis=1))
    o_ref[0] = jnp.concatenate(rows, axis=0) + b_ref[0]


def _ada_modulation(c, ada_w, ada_b, tn=512):
    depth, d, n3 = ada_w.shape
    nb = c.shape[0]
    return pl.pallas_call(
        functools.partial(_ada_kernel, nb=nb, tn=tn),
        grid=(depth, n3 // tn),
        in_specs=[pl.BlockSpec((d, nb), lambda l, j: (0, 0)),
                  pl.BlockSpec((1, d, tn), lambda l, j: (l, 0, j)),
                  pl.BlockSpec((1, 1, tn), lambda l, j: (l, 0, j))],
        out_specs=pl.BlockSpec((1, nb, tn), lambda l, j: (l, 0, j)),
        out_shape=jax.ShapeDtypeStruct((depth, nb, n3), F32),
        scratch_shapes=[pltpu.VMEM((nb, d, LANES), F32)],
        compiler_params=_params(2),
        name="ada_modulation",
    )(c.T, ada_w, ada_b.reshape(depth, 1, n3))


def _rms(v, g):
    var = jnp.mean(v * v, axis=-1, keepdims=True)
    return v * lax.rsqrt(var + EPS) * g


def _resnorm_kernel(*refs, has_post, has_pre, has_kv):
    it = iter(refs)
    x_ref = next(it)
    if has_post:
        y_ref, gate_ref, gpost_ref = next(it), next(it), next(it)
    if has_pre:
        shift_ref, scale_ref, gpre_ref = next(it), next(it), next(it)
    if has_kv:
        gkv_ref = next(it)
    x = x_ref[0]
    if has_post:
        x = x + gate_ref[0] * _rms(y_ref[0].astype(F32), gpost_ref[...])
        next(it)[0] = x
    if has_pre:
        h = _rms(x, gpre_ref[...]) * (1.0 + scale_ref[0]) + shift_ref[0]
        next(it)[0] = h.astype(BF16)
    if has_kv:
        next(it)[0] = _rms(x, gkv_ref[...]).astype(BF16)


def _resnorm(x, post=None, pre=None, kv_g=None, tr=256):
    bsz, s, d = x.shape
    row = pl.BlockSpec((1, tr, d), lambda b, i: (b, i, 0))
    per_b = pl.BlockSpec((1, 1, d), lambda b, i: (b, 0, 0))
    vec = pl.BlockSpec((1, d), lambda b, i: (0, 0))
    args, in_specs, out_specs, out_shape = [x], [row], [], []
    if post is not None:
        y, gate, g_post = post
        args += [y, gate, g_post.reshape(1, d)]
        in_specs += [row, per_b, vec]
        out_specs.append(row)
        out_shape.append(jax.ShapeDtypeStruct((bsz, s, d), F32))
    if pre is not None:
        shift, scale, g_pre = pre
        args += [shift, scale, g_pre.reshape(1, d)]
        in_specs += [per_b, per_b, vec]
        out_specs.append(row)
        out_shape.append(jax.ShapeDtypeStruct((bsz, s, d), BF16))
    if kv_g is not None:
        args.append(kv_g.reshape(1, d))
        in_specs.append(vec)
        out_specs.append(row)
        out_shape.append(jax.ShapeDtypeStruct((bsz, s, d), BF16))
    return pl.pallas_call(
        functools.partial(_resnorm_kernel, has_post=post is not None, has_pre=pre is not None,
                          has_kv=kv_g is not None),
        grid=(bsz, s // tr), in_specs=in_specs, out_specs=out_specs, out_shape=out_shape,
        compiler_params=_params(2), name="resnorm",
    )(*args)


class _CastJob(NamedTuple):
    w: jax.Array
    layer: int
    scale: Optional[jax.Array]


def _job_specs(jobs, ni, nj):
    nblk = min(SIDE_JOB_BLOCKS, 1 << ((ni * nj).bit_length() - 1))
    args, in_specs, out_specs, out_shape = [], [], [], []
    for job in jobs:
        _, r, c = job.w.shape
        rb = r // nblk
        blk = lambda i, j: jnp.minimum(i * nj + j, nblk - 1)
        args.append(job.w)
        in_specs.append(pl.BlockSpec((None, rb, c), lambda i, j, l=job.layer: (l, blk(i, j), 0)))
        if job.scale is not None:
            args.append(job.scale.reshape(1, c))
            in_specs.append(pl.BlockSpec((1, c), lambda i, j: (0, 0)))
        out_specs.append(pl.BlockSpec((rb, c), lambda i, j: (blk(i, j), 0)))
        out_shape.append(jax.ShapeDtypeStruct((r, c), BF16))
    return args, in_specs, out_specs, out_shape


def _run_cast_jobs(in_refs, out_refs, scaled):
    it = iter(in_refs)
    for o_ref, has_scale in zip(out_refs, scaled):
        w = next(it)[...]
        if has_scale:
            w = w * next(it)[...]
        o_ref[...] = w.astype(o_ref.dtype)


def _n_job_inputs(scaled):
    return len(scaled) + sum(scaled)


def _mm_kernel(*refs, has_bias, scaled):
    n_in = 2 + has_bias + _n_job_inputs(scaled)
    a_ref, b_ref = refs[:2]
    o_ref = refs[n_in]
    _run_cast_jobs(refs[2 + has_bias:n_in], refs[n_in + 1:], scaled)
    tn = o_ref.shape[1]
    sub = min(MXU_DIM, tn)
    for c in range(tn // sub):
        cs = slice(c * sub, (c + 1) * sub)
        acc = jnp.dot(a_ref[...], b_ref[:, cs], preferred_element_type=F32)
        if has_bias:
            acc = _softplus(acc + refs[2][:, cs])
        o_ref[:, cs] = acc.astype(o_ref.dtype)


def _matmul(a, b, layer, out_dtype, tm, tn, name, col0=0, ncols=None, bias=None, jobs=()):
    m, k = a.shape
    n = b.shape[2] - col0 if ncols is None else ncols
    tm, tn = min(tm, m), min(tn, n)
    jb0 = col0 // tn
    args = [a, b]
    in_specs = [pl.BlockSpec((tm, k), lambda i, j: (i, 0)),
                pl.BlockSpec((None, k, tn), lambda i, j: (layer, 0, jb0 + j))]
    if bias is not None:
        args.append(bias.reshape(1, n))
        in_specs.append(pl.BlockSpec((1, tn), lambda i, j: (0, j)))
    j_args, j_in, j_out, j_shape = _job_specs(jobs, m // tm, n // tn)
    out = pl.pallas_call(
        functools.partial(_mm_kernel, has_bias=bias is not None, scaled=tuple(jb.scale is not None for jb in jobs)),
        grid=(m // tm, n // tn),
        in_specs=in_specs + j_in,
        out_specs=[pl.BlockSpec((tm, tn), lambda i, j: (i, j))] + j_out,
        out_shape=[jax.ShapeDtypeStruct((m, n), out_dtype)] + j_shape,
        compiler_params=_params(2), name=name,
    )(*args, *j_args)
    return out if jobs else out[0]


CONV_HALO = 8


def _ssm_in_kernel(*refs, n_plain, tiles_per_seq, scaled):
    n_in = 4 + _n_job_inputs(scaled)
    a_ref, w_ref, cw_ref, cb_ref = refs[:4]
    o_ref, tail_ref = refs[n_in], refs[-1]
    _run_cast_jobs(refs[4:n_in], refs[n_in + 1:-1], scaled)
    i, j = pl.program_id(0), pl.program_id(1)
    tm, tn = o_ref.shape
    sub = min(MXU_DIM, tn)

    @pl.when((i == 0) & (j == 0))
    def _():
        tail_ref[...] = jnp.zeros_like(tail_ref)

    def cols(c):
        return slice(c * sub, (c + 1) * sub)

    def sub_dots(epilogue):
        for c in range(tn // sub):
            epilogue(jnp.dot(a_ref[...], w_ref[:, cols(c)], preferred_element_type=F32), cols(c))

    @pl.when(j < n_plain)
    def _():
        def plain(acc, cs):
            o_ref[:, cs] = acc.astype(o_ref.dtype)
        sub_dots(plain)

    @pl.when(j >= n_plain)
    def _():
        jc = j - n_plain
        seq_start = i % tiles_per_seq == 0
        nt = tm // CONV_HALO
        row = lax.broadcasted_iota(jnp.int32, (nt, CONV_HALO, sub), 1)

        def delayed(v, before, d):
            r = pltpu.roll(v, d, 1)
            prev = jnp.concatenate([pltpu.roll(before, d, 0)[None], r[:-1]], axis=0)
            return jnp.where(row < d, prev, r)

        def conv(acc, cs):
            tail = jnp.where(seq_start, 0.0, tail_ref[jc, :, cs])
            tail_ref[jc, :, cs] = acc[tm - CONV_HALO:]
            w = cw_ref[:, cs]
            w0, w1, w2, w3 = (w[k:k + 1] for k in range(SSM_CONV))
            a = acc.reshape(nt, CONV_HALO, sub)
            a1 = delayed(a, tail, 1)
            pair = w1 * a + w0 * a1
            pair_tail = w1 * tail + w0 * pltpu.roll(tail, 1, 0)
            half = cb_ref[:, cs] + w3 * a + w2 * a1 + delayed(pair, pair_tail, 2)
            o_ref[:, cs] = (half + half * jnp.tanh(half)).reshape(tm, sub).astype(o_ref.dtype)
        sub_dots(conv)


def _ssm_in_proj(h2, w_in, layer, conv_w, conv_b, n_plain_cols, seq, jobs=(), tm=1024, tn=1024):
    t, k = h2.shape
    c = conv_w.shape[1]
    n = n_plain_cols + c
    n_plain = n_plain_cols // tn
    conv_col = lambda i, j: (0, jnp.maximum(j - n_plain, 0))
    j_args, j_in, j_out, j_shape = _job_specs(jobs, t // tm, n // tn)
    return pl.pallas_call(
        functools.partial(_ssm_in_kernel, n_plain=n_plain, tiles_per_seq=seq // tm,
                          scaled=tuple(jb.scale is not None for jb in jobs)),
        grid=(t // tm, n // tn),
        in_specs=[pl.BlockSpec((tm, k), lambda i, j: (i, 0)),
                  pl.BlockSpec((None, k, tn), lambda i, j: (layer, 0, j)),
                  pl.BlockSpec((SSM_CONV, tn), conv_col),
                  pl.BlockSpec((1, tn), conv_col)] + j_in,
        out_specs=[pl.BlockSpec((tm, tn), lambda i, j: (i, j))] + j_out,
        out_shape=[jax.ShapeDtypeStruct((t, n), BF16)] + j_shape,
        scratch_shapes=[pltpu.VMEM((c // tn, CONV_HALO, tn), F32)],
        compiler_params=_params(2), name="ssm_in_proj",
    )(h2, w_in, 0.5 * conv_w, 0.5 * conv_b.reshape(1, c), *j_args)


def _scan_kernel(x_ref, b_ref, c_ref, z_ref, dta_ref, dtt_ref, alr_ref, alc_ref,
                 dexp_ref, ng_ref, ex_ref, o_ref, st_ref, *, nsub):
    L = CHUNK
    P = SSM_HEADDIM
    hp = LANES // P
    nheads = alr_ref.shape[2]
    npair = nheads // hp
    qh = MXU_DIM // P
    nquad = nheads // qh

    @pl.when(pl.program_id(2) == 0)
    def _():
        st_ref[...] = jnp.zeros_like(st_ref)

    a_r = -jnp.exp(alr_ref[0])
    a_c = -jnp.exp(alc_ref[0])
    lane = lax.broadcasted_iota(jnp.int32, (L, LANES), 1)
    sub = lax.broadcasted_iota(jnp.int32, (L, LANES), 0)
    lo = lane < P
    s_of_lane = lane & (P - 1)
    causal2 = s_of_lane <= sub
    tri = (lax.broadcasted_iota(jnp.int32, (L, L), 1) <= lax.broadcasted_iota(jnp.int32, (L, L), 0)).astype(BF16)
    tri_t2 = (sub <= s_of_lane).astype(BF16)
    blk = lax.broadcasted_iota(jnp.int32, (L, qh * P), 1) // P
    dexp = dexp_ref[...]
    ng = ng_ref[...]

    def pieces(v, n=3):
        out, rem = [], v
        for _ in range(n):
            hi = rem.astype(BF16).astype(F32)
            out.append(hi)
            rem = rem - hi
        return out

    def local(ci):
        rows = slice(ci * L, (ci + 1) * L)
        x = x_ref[0, rows, :]
        bm = b_ref[0, rows, :]
        cm = c_ref[0, rows, :]
        dt = dta_ref[0, ci, 0]
        dt_t = dtt_ref[0, ci, 0]
        acum = sum(jnp.dot(tri, p.astype(BF16), preferred_element_type=F32) for p in pieces(dt * a_r))
        acum_t = sum(jnp.dot(p.astype(BF16), tri_t2, preferred_element_type=F32)
                     for p in pieces(dt_t[:, :L] * a_c))
        e_out = jnp.exp(acum[L - 1:L, :] - acum) * dt
        ex = ex_ref[...]
        spr = lambda v, n: jnp.dot(jnp.concatenate(pieces(v, n), axis=1).astype(BF16), ex[:n * nheads],
                                   preferred_element_type=F32)
        col, sc_out = spr(acum, 3), spr(e_out, 2)
        sc_in = jnp.exp(col)
        cb2 = lax.dot_general(cm, jnp.concatenate([bm, bm], axis=0), (((1,), (1,)), ((), ())),
                              preferred_element_type=F32)
        xf = x.astype(F32)

        w_tiles = []
        for j in range(npair):
            r0, r1 = hp * j, hp * j + 1
            row = jnp.where(lo[0:1], acum_t[r0:r0 + 1], acum_t[r1:r1 + 1])
            dtr = jnp.where(lo[0:1], dt_t[r0:r0 + 1], dt_t[r1:r1 + 1])
            seg = jnp.where(causal2, col[:, j * LANES:(j + 1) * LANES] - row, NEG)
            w_tiles.append((jnp.exp(seg) * (cb2 * dtr)).astype(BF16))

        y_parts = []
        for q in range(nquad):
            w4 = jnp.concatenate(w_tiles[q * qh // hp:(q + 1) * qh // hp], axis=1)
            x4 = x[:, q * qh * P:(q + 1) * qh * P]
            xbd = jnp.concatenate([jnp.where(blk == i, x4, jnp.zeros_like(x4)) for i in range(qh)], axis=0)
            y_parts.append(jnp.dot(w4, xbd, preferred_element_type=F32))
        y_loc = jnp.concatenate(y_parts, axis=1) + xf * dexp

        xw = x * sc_out.astype(BF16)
        upd = lax.dot_general(bm, xw, (((0,), (0,)), ((), ())), preferred_element_type=F32)
        return cm, sc_in, y_loc, upd

    def carried(ci, st, cm, sc_in, y_loc, upd):
        rows = slice(ci * L, (ci + 1) * L)
        y = y_loc + jnp.dot(cm, st.astype(BF16), preferred_element_type=F32) * sc_in
        yg = y * _silu_of_twice(z_ref[0, rows, :].astype(F32))
        o_ref[0, rows, :] = _rms(yg, ng).astype(o_ref.dtype)
        return st * sc_in[L - 1:L, :] + upd

    st = st_ref[...]
    for ci in range(nsub):
        st = carried(ci, st, *local(ci))
    st_ref[...] = st


def _spread_matrix(nheads_g, parts=3):
    rows = np.arange(parts * nheads_g)
    cols = np.arange(nheads_g * SSM_HEADDIM)
    return jnp.asarray((rows % nheads_g)[:, None] == (cols // SSM_HEADDIM)[None, :], dtype=BF16)


def _ssd_scan(zx, dt, a_log, d_skip, norm_g, nsub=16):
    bsz, s, _ = zx.shape
    g = SSM_GROUPS
    nheads = dt.shape[-1]
    r = nheads // g
    gw = r * SSM_HEADDIM
    d_inner = nheads * SSM_HEADDIM
    nc = s // CHUNK
    lb = nsub * CHUNK
    dt5 = dt.reshape(bsz, nc, CHUNK, g, r)
    dt_a = dt5.transpose(0, 1, 3, 2, 4)
    dt_t = dt5.transpose(0, 1, 3, 4, 2)
    dt_t = jnp.concatenate([dt_t, dt_t], axis=-1)
    x_off = d_inner // gw
    b_off = 2 * d_inner // SSM_STATE
    spread = _spread_matrix(r)
    row_spec = pl.BlockSpec((1, 1, r), lambda b, gi, ci: (gi, 0, 0))
    col_spec = pl.BlockSpec((1, r, 1), lambda b, gi, ci: (gi, 0, 0))
    wide = lambda off: pl.BlockSpec((1, lb, gw), lambda b, gi, ci: (b, ci, off + gi))
    narrow = lambda off: pl.BlockSpec((1, lb, SSM_STATE), lambda b, gi, ci: (b, ci, off + gi))
    return pl.pallas_call(
        functools.partial(_scan_kernel, nsub=nsub),
        grid=(bsz, g, nc // nsub),
        in_specs=[wide(x_off), narrow(b_off), narrow(b_off + g), wide(0),
                  pl.BlockSpec((1, nsub, 1, CHUNK, r), lambda b, gi, ci: (b, ci, gi, 0, 0)),
                  pl.BlockSpec((1, nsub, 1, r, 2 * CHUNK), lambda b, gi, ci: (b, ci, gi, 0, 0)),
                  row_spec, col_spec,
                  pl.BlockSpec((1, gw), lambda b, gi, ci: (0, gi)),
                  pl.BlockSpec((1, gw), lambda b, gi, ci: (0, gi)),
                  pl.BlockSpec(spread.shape, lambda b, gi, ci: (0, 0))],
        out_specs=wide(0),
        out_shape=jax.ShapeDtypeStruct((bsz, s, d_inner), BF16),
        scratch_shapes=[pltpu.VMEM((SSM_STATE, gw), F32)],
        compiler_params=_params(3), name="ssd_scan",
    )(zx, zx, zx, zx, dt_a, dt_t, a_log.reshape(g, 1, r), a_log.reshape(g, r, 1),
      jnp.repeat(d_skip, SSM_HEADDIM).reshape(1, d_inner), norm_g.reshape(1, d_inner), spread)


def _attn_kernel(q_ref, z_ref, k0_ref, k1_ref, k2_ref, v0_ref, v1_ref, v2_ref, rv_ref, o_ref, bias_sc, *, hb):
    i = pl.program_id(2)

    @pl.when((pl.program_id(1) == 0) & (i == 0))
    def _():
        l = lax.broadcasted_iota(jnp.int32, (ATT_QBLK, ATT_KBLK), 0)
        j = lax.broadcasted_iota(jnp.int32, (ATT_QBLK, ATT_KBLK), 1)
        lc, jc = l // CHUNK, j // CHUNK
        band = (jc >= lc) & (jc <= lc + ATT_LEFT_CHUNKS)
        for h in range(hb):
            src = jnp.broadcast_to(rv_ref[h], (ATT_QBLK, ATT_RVEC))
            toe = pltpu.roll(src, ATT_RVEC - ATT_QBLK, 1, stride=1, stride_axis=0)[:, :ATT_KBLK]
            for v in range(ATT_VARIANTS):
                ok = band & (j >= (ATT_KBLK - ATT_QBLK) - ATT_QBLK * v)
                b = jnp.where(ok, toe, NEG)
                for half in range(2):
                    bias_sc[v, h, half] = b[half * ATT_HALF:(half + 1) * ATT_HALF,
                                            half * ATT_HALF:half * ATT_HALF + ATT_WIN]

    v = jnp.minimum(i, ATT_VARIANTS - 1)
    units = [(h, half) for h in range(hb) for half in range(2)]

    def lanes(h):
        return slice(h * ATT_HEADDIM, (h + 1) * ATT_HEADDIM)

    def rows(half):
        return slice(half * ATT_HALF, (half + 1) * ATT_HALF)

    def window(refs, h, half):
        cat = jnp.concatenate([r[0, :, lanes(h)] for r in refs], axis=0)
        return cat[half * ATT_HALF:half * ATT_HALF + ATT_WIN]

    def scores(h, half):
        s = lax.dot_general(q_ref[0, rows(half), lanes(h)], window((k0_ref, k1_ref, k2_ref), h, half),
                            (((1,), (1,)), ((), ())), preferred_element_type=F32)
        return s + bias_sc[v, h, half]

    def probs(s):
        p = jnp.exp2(s - jnp.max(s, axis=1, keepdims=True))
        return p.astype(BF16), jnp.sum(p, axis=1, keepdims=True)

    def finish(h, half, p, den):
        o = jnp.dot(p, window((v0_ref, v1_ref, v2_ref), h, half), preferred_element_type=F32) / den
        gate = _silu_of_twice(z_ref[0, rows(half), lanes(h)].astype(F32))
        o_ref[0, rows(half), lanes(h)] = (o * gate).astype(o_ref.dtype)

    stages = [lambda u, _: scores(*units[u]), lambda u, s: probs(s), lambda u, pd: finish(*units[u], *pd)]
    inflight = {}
    n = len(units)
    for step in range(n + len(stages) - 1):
        for k, stage in enumerate(stages):
            u = step - k
            if 0 <= u < n:
                inflight[u] = stage(u, inflight.get(u))


def _rel_source_rows(rel_table):
    u = np.arange(ATT_RVEC)
    bucket = np.clip(ATT_KBLK - u, -REL_MAX, REL_MAX) + REL_MAX
    return (rel_table[:, bucket].astype(F32) * LOG2E)[:, None, :]


def _attention(qz, kv, rel_rows, hb=16):
    bsz, s, two_d = qz.shape
    d = two_d // 2
    hw = hb * ATT_HEADDIM
    ng = d // hw
    cur = lambda off: pl.BlockSpec((1, ATT_QBLK, hw), lambda hg, b, i: (b, i, off + hg))
    back = lambda off, n: pl.BlockSpec((1, ATT_QBLK, hw), lambda hg, b, i: (b, jnp.maximum(i - n, 0), off + hg))
    return pl.pallas_call(
        functools.partial(_attn_kernel, hb=hb),
        grid=(ng, bsz, s // ATT_QBLK),
        in_specs=[cur(0), cur(ng), back(0, 2), back(0, 1), cur(0), back(ng, 2), back(ng, 1), cur(ng),
                  pl.BlockSpec((hb, 1, ATT_RVEC), lambda hg, b, i: (hg, 0, 0))],
        out_specs=pl.BlockSpec((1, ATT_QBLK, hw), lambda hg, b, i: (b, i, hg)),
        out_shape=jax.ShapeDtypeStruct((bsz, s, d), BF16),
        scratch_shapes=[pltpu.VMEM((ATT_VARIANTS, hb, 2, ATT_HALF, ATT_WIN), F32)],
        compiler_params=_params(3), name="band_attention",
    )(qz, qz, kv, kv, kv, kv, kv, kv, rel_rows)


def kernel(x, c, ada_w, ada_b, pre_norm_g, post_norm_g, ssm_w_in, ssm_conv_w, ssm_conv_b, ssm_dt_bias, ssm_a_log, ssm_d, ssm_norm_g, ssm_w_out, kv_norm_g, w_kv, att_w_in, att_rel_bias, att_w_out):
    bsz, s, d = x.shape
    depth = ada_w.shape[0]
    n_a = ssm_w_in.shape[0]
    t = bsz * s
    d_inner = ssm_w_out.shape[1]
    nheads = ssm_dt_bias.shape[1]
    zx_cols = ssm_w_in.shape[2] - nheads

    mod = _ada_modulation(c, ada_w, ada_b)
    shift = [mod[l, :, None, 0:d] for l in range(depth)]
    scale = [mod[l, :, None, d:2 * d] for l in range(depth)]
    gate = [mod[l, :, None, 2 * d:3 * d] for l in range(depth)]

    att_d = att_w_out.shape[1]
    qscale = jnp.concatenate([jnp.full((att_d,), ATT_HEADDIM ** -0.5 * LOG2E, F32), jnp.full((att_d,), 0.5, F32)])
    zscale = jnp.concatenate([jnp.full((d_inner,), 0.5, F32), jnp.ones((ssm_w_in.shape[2] - d_inner,), F32)])
    n_b = depth - n_a
    w_in_b = (ssm_w_in[0:1] * zscale).astype(BF16)
    w_out_b = w_kv_b = None
    att_in_b, att_out_b = [None] * n_b, [None] * n_b

    (h,) = _resnorm(x, pre=(shift[0], scale[0], pre_norm_g[0]))
    kv = None
    for layer in range(depth):
        h2 = h.reshape(t, d)
        if layer < n_a:
            i = layer
            jobs = [_CastJob(ssm_w_out, i, None)]
            if i + 1 < n_a:
                jobs.append(_CastJob(ssm_w_in, i + 1, zscale))
            else:
                jobs += [_CastJob(w_kv[None], 0, None), _CastJob(att_w_in, 0, qscale)]
            zx, w_out_b, *rest = _ssm_in_proj(h2, w_in_b, 0, ssm_conv_w[i], ssm_conv_b[i], d_inner, s, jobs=jobs)
            dt = _matmul(h2, w_in_b, 0, F32, 1024, LANES, "ssm_dt_proj", col0=zx_cols, ncols=nheads,
                         bias=ssm_dt_bias[i])
            if i + 1 < n_a:
                w_in_b = rest[0][None]
            else:
                w_kv_b, att_in_b[0] = rest
            yn = _ssd_scan(zx.reshape(bsz, s, zx_cols), dt.reshape(bsz, s, nheads), ssm_a_log[i], ssm_d[i],
                           ssm_norm_g[i])
            y = _matmul(yn.reshape(t, d_inner), w_out_b[None], 0, BF16, 1024, 512, "ssm_out_proj")
        else:
            i = layer - n_a
            qz = _matmul(h2, att_in_b[i][None], 0, BF16, 1024, 1024, "att_in_proj")
            o = _attention(qz.reshape(bsz, s, 2 * att_d), kv, _rel_source_rows(att_rel_bias[i]))
            y = _matmul(o.reshape(t, att_d), att_out_b[i][None], 0, BF16, 1024, 1024, "att_out_proj")
        post = (y.reshape(bsz, s, d), gate[layer], post_norm_g[layer])
        if layer + 1 < depth:
            nxt = layer + 1
            pre = (shift[nxt], scale[nxt], pre_norm_g[nxt])
            if nxt == n_a:
                x, h, hkv = _resnorm(x, post=post, pre=pre, kv_g=kv_norm_g)
                jobs = [_CastJob(att_w_out, 0, None)]
                for bi in range(1, n_b):
                    jobs += [_CastJob(att_w_in, bi, qscale), _CastJob(att_w_out, bi, None)]
                kv, att_out_b[0], *rest = _matmul(hkv.reshape(t, d), w_kv_b[None], 0, BF16, 1024, 1024, "kv_proj",
                                                  jobs=jobs)
                att_in_b[1:], att_out_b[1:] = rest[0::2], rest[1::2]
                kv = kv.reshape(bsz, s, w_kv.shape[1])
            else:
                x, h = _resnorm(x, post=post, pre=pre)
        else:
            (x,) = _resnorm(x, post=post)
    return x
```

```python
import functools
from typing import NamedTuple, Optional

import jax
import jax.numpy as jnp
import numpy as np
from jax import lax
from jax.experimental import pallas as pl
from jax.experimental.pallas import tpu as pltpu

F32 = jnp.float32
BF16 = jnp.bfloat16

EPS = 1e-6
CHUNK = 64
LANES = 128
MXU_DIM = 256
V7X_VMEM_BYTES = 64 * 1024 * 1024
VMEM_LIMIT = V7X_VMEM_BYTES - 8 * 1024 * 1024

SSM_HEADDIM = 64
SSM_GROUPS = 8
SSM_STATE = 128
SSM_CONV = 4
ATT_HEADDIM = 128
ATT_LEFT_CHUNKS = 8
REL_MAX = 128
ATT_QBLK = 4 * CHUNK
ATT_KBLK = 3 * ATT_QBLK
ATT_HALF = ATT_QBLK // 2
ATT_WIN = ATT_KBLK - ATT_HALF
ATT_RVEC = 1024
ATT_VARIANTS = 3
LOG2E = 1.4426950408889634
NEG = -1e30
SIDE_JOB_BLOCKS = 256


def _params(n_grid, vmem=VMEM_LIMIT):
    return pltpu.CompilerParams(dimension_semantics=("arbitrary",) * n_grid, vmem_limit_bytes=vmem)


def _silu_of_twice(h):
    return h + h * jnp.tanh(h)


def _silu(v):
    return _silu_of_twice(0.5 * v)


def _softplus(v):
    return jnp.maximum(v, 0.0) + jnp.log1p(jnp.exp(-jnp.abs(v)))


def _ada_kernel(ct_ref, w_ref, b_ref, o_ref, sb_ref, *, nb, tn):
    d = ct_ref.shape[0]

    @pl.when((pl.program_id(0) == 0) & (pl.program_id(1) == 0))
    def _():
        s = _silu(ct_ref[...])
        for b in range(nb):
            sb_ref[b] = jnp.broadcast_to(s[:, b:b + 1], (d, LANES))

    rows = []
    for b in range(nb):
        cols = []
        for t in range(tn // LANES):
            w = w_ref[0, :, t * LANES:(t + 1) * LANES]
            cols.append(jnp.sum(w * sb_ref[b], axis=0, keepdims=True))
        rows.append(jnp.concatenate(cols, axis=1))
    o_ref[0] = jnp.concatenate(rows, axis=0) + b_ref[0]


def _ada_modulation(c, ada_w, ada_b, tn=512):
    depth, d, n3 = ada_w.shape
    nb = c.shape[0]
    return pl.pallas_call(
        functools.partial(_ada_kernel, nb=nb, tn=tn),
        grid=(depth, n3 // tn),
        in_specs=[pl.BlockSpec((d, nb), lambda l, j: (0, 0)),
                  pl.BlockSpec((1, d, tn), lambda l, j: (l, 0, j)),
                  pl.BlockSpec((1, 1, tn), lambda l, j: (l, 0, j))],
        out_specs=pl.BlockSpec((1, nb, tn), lambda l, j: (l, 0, j)),
        out_shape=jax.ShapeDtypeStruct((depth, nb, n3), F32),
        scratch_shapes=[pltpu.VMEM((nb, d, LANES), F32)],
        compiler_params=_params(2),
        name="ada_modulation",
    )(c.T, ada_w, ada_b.reshape(depth, 1, n3))


def _rms(v, g):
    var = jnp.mean(v * v, axis=-1, keepdims=True)
    return v * lax.rsqrt(var + EPS) * g


def _resnorm_kernel(*refs, has_post, has_pre, has_kv):
    it = iter(refs)
    x_ref = next(it)
    if has_post:
        y_ref, gate_ref, gpost_ref = next(it), next(it), next(it)
    if has_pre:
        shift_ref, scale_ref, gpre_ref = next(it), next(it), next(it)
    if has_kv:
        gkv_ref = next(it)
    x = x_ref[0]
    if has_post:
        x = x + gate_ref[0] * _rms(y_ref[0].astype(F32), gpost_ref[...])
        next(it)[0] = x
    if has_pre:
        h = _rms(x, gpre_ref[...]) * (1.0 + scale_ref[0]) + shift_ref[0]
        next(it)[0] = h.astype(BF16)
    if has_kv:
        next(it)[0] = _rms(x, gkv_ref[...]).astype(BF16)


def _resnorm(x, post=None, pre=None, kv_g=None, tr=256):
    bsz, s, d = x.shape
    row = pl.BlockSpec((1, tr, d), lambda b, i: (b, i, 0))
    per_b = pl.BlockSpec((1, 1, d), lambda b, i: (b, 0, 0))
    vec = pl.BlockSpec((1, d), lambda b, i: (0, 0))
    args, in_specs, out_specs, out_shape = [x], [row], [], []
    if post is not None:
        y, gate, g_post = post
        args += [y, gate, g_post.reshape(1, d)]
        in_specs += [row, per_b, vec]
        out_specs.append(row)
        out_shape.append(jax.ShapeDtypeStruct((bsz, s, d), F32))
    if pre is not None:
        shift, scale, g_pre = pre
        args += [shift, scale, g_pre.reshape(1, d)]
        in_specs += [per_b, per_b, vec]
        out_specs.append(row)
        out_shape.append(jax.ShapeDtypeStruct((bsz, s, d), BF16))
    if kv_g is not None:
        args.append(kv_g.reshape(1, d))
        in_specs.append(vec)
        out_specs.append(row)
        out_shape.append(jax.ShapeDtypeStruct((bsz, s, d), BF16))
    return pl.pallas_call(
        functools.partial(_resnorm_kernel, has_post=post is not None, has_pre=pre is not None,
                          has_kv=kv_g is not None),
        grid=(bsz, s // tr), in_specs=in_specs, out_specs=out_specs, out_shape=out_shape,
        compiler_params=_params(2), name="resnorm",
    )(*args)


class _CastJob(NamedTuple):
    w: jax.Array
    layer: int
    scale: Optional[jax.Array]


def _job_specs(jobs, ni, nj):
    nblk = min(SIDE_JOB_BLOCKS, 1 << ((ni * nj).bit_length() - 1))
    args, in_specs, out_specs, out_shape = [], [], [], []
    for job in jobs:
        _, r, c = job.w.shape
        rb = r // nblk
        blk = lambda i, j: jnp.minimum(i * nj + j, nblk - 1)
        args.append(job.w)
        in_specs.append(pl.BlockSpec((None, rb, c), lambda i, j, l=job.layer: (l, blk(i, j), 0)))
        if job.scale is not None:
            args.append(job.scale.reshape(1, c))
            in_specs.append(pl.BlockSpec((1, c), lambda i, j: (0, 0)))
        out_specs.append(pl.BlockSpec((rb, c), lambda i, j: (blk(i, j), 0)))
        out_shape.append(jax.ShapeDtypeStruct((r, c), BF16))
    return args, in_specs, out_specs, out_shape


def _run_cast_jobs(in_refs, out_refs, scaled):
    it = iter(in_refs)
    for o_ref, has_scale in zip(out_refs, scaled):
        w = next(it)[...]
        if has_scale:
            w = w * next(it)[...]
        o_ref[...] = w.astype(o_ref.dtype)


def _n_job_inputs(scaled):
    return len(scaled) + sum(scaled)


def _mm_kernel(*refs, has_bias, scaled):
    n_in = 2 + has_bias + _n_job_inputs(scaled)
    a_ref, b_ref = refs[:2]
    o_ref = refs[n_in]
    _run_cast_jobs(refs[2 + has_bias:n_in], refs[n_in + 1:], scaled)
    tn = o_ref.shape[1]
    sub = min(MXU_DIM, tn)
    for c in range(tn // sub):
        cs = slice(c * sub, (c + 1) * sub)
        acc = jnp.dot(a_ref[...], b_ref[:, cs], preferred_element_type=F32)
        if has_bias:
            acc = _softplus(acc + refs[2][:, cs])
        o_ref[:, cs] = acc.astype(o_ref.dtype)


def _matmul(a, b, layer, out_dtype, tm, tn, name, col0=0, ncols=None, bias=None, jobs=()):
    m, k = a.shape
    n = b.shape[2] - col0 if ncols is None else ncols
    tm, tn = min(tm, m), min(tn, n)
    jb0 = col0 // tn
    args = [a, b]
    in_specs = [pl.BlockSpec((tm, k), lambda i, j: (i, 0)),
                pl.BlockSpec((None, k, tn), lambda i, j: (layer, 0, jb0 + j))]
    if bias is not None:
        args.append(bias.reshape(1, n))
        in_specs.append(pl.BlockSpec((1, tn), lambda i, j: (0, j)))
    j_args, j_in, j_out, j_shape = _job_specs(jobs, m // tm, n // tn)
    out = pl.pallas_call(
        functools.partial(_mm_kernel, has_bias=bias is not None, scaled=tuple(jb.scale is not None for jb in jobs)),
        grid=(m // tm, n // tn),
        in_specs=in_specs + j_in,
        out_specs=[pl.BlockSpec((tm, tn), lambda i, j: (i, j))] + j_out,
        out_shape=[jax.ShapeDtypeStruct((m, n), out_dtype)] + j_shape,
        compiler_params=_params(2), name=name,
    )(*args, *j_args)
    return out if jobs else out[0]


CONV_HALO = 8


def _ssm_in_kernel(*refs, n_plain, tiles_per_seq, scaled):
    n_in = 4 + _n_job_inputs(scaled)
    a_ref, w_ref, cw_ref, cb_ref = refs[:4]
    o_ref, tail_ref = refs[n_in], refs[-1]
    _run_cast_jobs(refs[4:n_in], refs[n_in + 1:-1], scaled)
    i, j = pl.program_id(0), pl.program_id(1)
    tm, tn = o_ref.shape
    sub = min(MXU_DIM, tn)

    @pl.when((i == 0) & (j == 0))
    def _():
        tail_ref[...] = jnp.zeros_like(tail_ref)

    def cols(c):
        return slice(c * sub, (c + 1) * sub)

    def sub_dots(epilogue):
        for c in range(tn // sub):
            epilogue(jnp.dot(a_ref[...], w_ref[:, cols(c)], preferred_element_type=F32), cols(c))

    @pl.when(j < n_plain)
    def _():
        def plain(acc, cs):
            o_ref[:, cs] = acc.astype(o_ref.dtype)
        sub_dots(plain)

    @pl.when(j >= n_plain)
    def _():
        jc = j - n_plain
        seq_start = i % tiles_per_seq == 0
        nt = tm // CONV_HALO
        row = lax.broadcasted_iota(jnp.int32, (nt, CONV_HALO, sub), 1)

        def delayed(v, before, d):
            r = pltpu.roll(v, d, 1)
            prev = jnp.concatenate([pltpu.roll(before, d, 0)[None], r[:-1]], axis=0)
            return jnp.where(row < d, prev, r)

        def conv(acc, cs):
            tail = jnp.where(seq_start, 0.0, tail_ref[jc, :, cs])
            tail_ref[jc, :, cs] = acc[tm - CONV_HALO:]
            w = cw_ref[:, cs]
            w0, w1, w2, w3 = (w[k:k + 1] for k in range(SSM_CONV))
            a = acc.reshape(nt, CONV_HALO, sub)
            a1 = delayed(a, tail, 1)
            pair = w1 * a + w0 * a1
            pair_tail = w1 * tail + w0 * pltpu.roll(tail, 1, 0)
            half = cb_ref[:, cs] + w3 * a + w2 * a1 + delayed(pair, pair_tail, 2)
            o_ref[:, cs] = (half + half * jnp.tanh(half)).reshape(tm, sub).astype(o_ref.dtype)
        sub_dots(conv)


def _ssm_in_proj(h2, w_in, layer, conv_w, conv_b, n_plain_cols, seq, jobs=(), tm=1024, tn=1024):
    t, k = h2.shape
    c = conv_w.shape[1]
    n = n_plain_cols + c
    n_plain = n_plain_cols // tn
    conv_col = lambda i, j: (0, jnp.maximum(j - n_plain, 0))
    j_args, j_in, j_out, j_shape = _job_specs(jobs, t // tm, n // tn)
    return pl.pallas_call(
        functools.partial(_ssm_in_kernel, n_plain=n_plain, tiles_per_seq=seq // tm,
                          scaled=tuple(jb.scale is not None for jb in jobs)),
        grid=(t // tm, n // tn),
        in_specs=[pl.BlockSpec((tm, k), lambda i, j: (i, 0)),
                  pl.BlockSpec((None, k, tn), lambda i, j: (layer, 0, j)),
                  pl.BlockSpec((SSM_CONV, tn), conv_col),
                  pl.BlockSpec((1, tn), conv_col)] + j_in,
        out_specs=[pl.BlockSpec((tm, tn), lambda i, j: (i, j))] + j_out,
        out_shape=[jax.ShapeDtypeStruct((t, n), BF16)] + j_shape,
        scratch_shapes=[pltpu.VMEM((c // tn, CONV_HALO, tn), F32)],
        compiler_params=_params(2), name="ssm_in_proj",
    )(h2, w_in, 0.5 * conv_w, 0.5 * conv_b.reshape(1, c), *j_args)


def _scan_kernel(x_ref, b_ref, c_ref, z_ref, dta_ref, dtt_ref, alr_ref, alc_ref,
                 dexp_ref, ng_ref, ex_ref, o_ref, st_ref, *, nsub):
    L = CHUNK
    P = SSM_HEADDIM
    hp = LANES // P
    nheads = alr_ref.shape[2] // nsub
    npair = nheads // hp
    qh = MXU_DIM // P
    nquad = nheads // qh

    @pl.when(pl.program_id(2) == 0)
    def _():
        st_ref[...] = jnp.zeros_like(st_ref)

    a_r = -jnp.exp(alr_ref[0])
    a_c = -jnp.exp(alc_ref[0])
    lane = lax.broadcasted_iota(jnp.int32, (L, LANES), 1)
    sub = lax.broadcasted_iota(jnp.int32, (L, LANES), 0)
    lo = lane < P
    s_of_lane = lane & (P - 1)
    causal2 = s_of_lane <= sub
    tri = (lax.broadcasted_iota(jnp.int32, (L, L), 1) <= lax.broadcasted_iota(jnp.int32, (L, L), 0)).astype(BF16)
    tri_t2 = (sub <= s_of_lane).astype(BF16)
    blk = lax.broadcasted_iota(jnp.int32, (L, qh * P), 1) // P
    dexp = dexp_ref[...]
    ng = ng_ref[...]

    def pieces(v, n=3):
        out, rem = [], v
        for _ in range(n):
            hi = rem.astype(BF16).astype(F32)
            out.append(hi)
            rem = rem - hi
        return out

    dt_all = dta_ref[0, 0, 0]
    dt_t_all = dtt_ref[0, 0, 0]
    acum_all = sum(jnp.dot(tri, p.astype(BF16), preferred_element_type=F32) for p in pieces(dt_all * a_r))
    acum_t_all = sum(jnp.dot(p.astype(BF16), tri_t2, preferred_element_type=F32)
                     for p in pieces(dt_t_all[:, :L] * a_c))
    e_out_all = jnp.exp(acum_all[L - 1:L, :] - acum_all) * dt_all
    acum_pieces, e_out_pieces = pieces(acum_all, 3), pieces(e_out_all, 2)
    ex = ex_ref[...]

    def spread(pcs, ci):
        lhs = jnp.concatenate([p[:, ci * nheads:(ci + 1) * nheads] for p in pcs], axis=1).astype(BF16)
        return jnp.dot(lhs, ex[:len(pcs) * nheads], preferred_element_type=F32)

    def local(ci):
        rows = slice(ci * L, (ci + 1) * L)
        x = x_ref[0, rows, :]
        bm = b_ref[0, rows, :]
        cm = c_ref[0, rows, :]
        dt_t = dt_t_all[ci * nheads:(ci + 1) * nheads]
        acum_t = acum_t_all[ci * nheads:(ci + 1) * nheads]
        col, sc_out = spread(acum_pieces, ci), spread(e_out_pieces, ci)
        sc_in = jnp.exp(col)
        cb2 = lax.dot_general(cm, jnp.concatenate([bm, bm], axis=0), (((1,), (1,)), ((), ())),
                              preferred_element_type=F32)
        xf = x.astype(F32)

        w_tiles = []
        for j in range(npair):
            r0, r1 = hp * j, hp * j + 1
            row = jnp.where(lo[0:1], acum_t[r0:r0 + 1], acum_t[r1:r1 + 1])
            dtr = jnp.where(lo[0:1], dt_t[r0:r0 + 1], dt_t[r1:r1 + 1])
            seg = jnp.where(causal2, col[:, j * LANES:(j + 1) * LANES] - row, NEG)
            w_tiles.append((jnp.exp(seg) * (cb2 * dtr)).astype(BF16))

        y_parts = []
        for q in range(nquad):
            w4 = jnp.concatenate(w_tiles[q * qh // hp:(q + 1) * qh // hp], axis=1)
            x4 = x[:, q * qh * P:(q + 1) * qh * P]
            xbd = jnp.concatenate([jnp.where(blk == i, x4, jnp.zeros_like(x4)) for i in range(qh)], axis=0)
            y_parts.append(jnp.dot(w4, xbd, preferred_element_type=F32))
        y_loc = jnp.concatenate(y_parts, axis=1) + xf * dexp

        xw = x * sc_out.astype(BF16)
        upd = lax.dot_general(bm, xw, (((0,), (0,)), ((), ())), preferred_element_type=F32)
        return cm, sc_in, y_loc, upd

    def carried(ci, st, cm, sc_in, y_loc, upd):
        rows = slice(ci * L, (ci + 1) * L)
        y = y_loc + jnp.dot(cm, st.astype(BF16), preferred_element_type=F32) * sc_in
        yg = y * _silu_of_twice(z_ref[0, rows, :].astype(F32))
        o_ref[0, rows, :] = _rms(yg, ng).astype(o_ref.dtype)
        return st * sc_in[L - 1:L, :] + upd

    st = st_ref[...]
    for ci in range(nsub):
        st = carried(ci, st, *local(ci))
    st_ref[...] = st


def _spread_matrix(nheads_g, parts=3):
    rows = np.arange(parts * nheads_g)
    cols = np.arange(nheads_g * SSM_HEADDIM)
    return jnp.asarray((rows % nheads_g)[:, None] == (cols // SSM_HEADDIM)[None, :], dtype=BF16)


def _ssd_scan(zx, dt, a_log, d_skip, norm_g, nsub=16):
    bsz, s, _ = zx.shape
    g = SSM_GROUPS
    nheads = dt.shape[-1]
    r = nheads // g
    gw = r * SSM_HEADDIM
    d_inner = nheads * SSM_HEADDIM
    nc = s // CHUNK
    lb = nsub * CHUNK
    nstep = nc // nsub
    dt6 = dt.reshape(bsz, nstep, nsub, CHUNK, g, r)
    dt_a = dt6.transpose(0, 4, 1, 3, 2, 5).reshape(bsz, g, nstep, CHUNK, nsub * r)
    dt_t = dt6.transpose(0, 4, 1, 2, 5, 3).reshape(bsz, g, nstep, nsub * r, CHUNK)
    dt_t = jnp.concatenate([dt_t, dt_t], axis=-1)
    x_off = d_inner // gw
    b_off = 2 * d_inner // SSM_STATE
    spread = _spread_matrix(r)
    row_spec = pl.BlockSpec((1, 1, nsub * r), lambda b, gi, ci: (gi, 0, 0))
    col_spec = pl.BlockSpec((1, nsub * r, 1), lambda b, gi, ci: (gi, 0, 0))
    a_log_steps = jnp.tile(a_log.reshape(g, 1, r), (1, nsub, 1))
    wide = lambda off: pl.BlockSpec((1, lb, gw), lambda b, gi, ci: (b, ci, off + gi))
    narrow = lambda off: pl.BlockSpec((1, lb, SSM_STATE), lambda b, gi, ci: (b, ci, off + gi))
    return pl.pallas_call(
        functools.partial(_scan_kernel, nsub=nsub),
        grid=(bsz, g, nstep),
        in_specs=[wide(x_off), narrow(b_off), narrow(b_off + g), wide(0),
                  pl.BlockSpec((1, 1, 1, CHUNK, nsub * r), lambda b, gi, ci: (b, gi, ci, 0, 0)),
                  pl.BlockSpec((1, 1, 1, nsub * r, 2 * CHUNK), lambda b, gi, ci: (b, gi, ci, 0, 0)),
                  row_spec, col_spec,
                  pl.BlockSpec((1, gw), lambda b, gi, ci: (0, gi)),
                  pl.BlockSpec((1, gw), lambda b, gi, ci: (0, gi)),
                  pl.BlockSpec(spread.shape, lambda b, gi, ci: (0, 0))],
        out_specs=wide(0),
        out_shape=jax.ShapeDtypeStruct((bsz, s, d_inner), BF16),
        scratch_shapes=[pltpu.VMEM((SSM_STATE, gw), F32)],
        compiler_params=_params(3), name="ssd_scan",
    )(zx, zx, zx, zx, dt_a, dt_t, a_log_steps.reshape(g, 1, nsub * r), a_log_steps.reshape(g, nsub * r, 1),
      jnp.repeat(d_skip, SSM_HEADDIM).reshape(1, d_inner), norm_g.reshape(1, d_inner), spread)


def _attn_kernel(q_ref, z_ref, k0_ref, k1_ref, k2_ref, v0_ref, v1_ref, v2_ref, rv_ref, o_ref, bias_sc, *, hb):
    i = pl.program_id(2)

    @pl.when((pl.program_id(1) == 0) & (i == 0))
    def _():
        l = lax.broadcasted_iota(jnp.int32, (ATT_QBLK, ATT_KBLK), 0)
        j = lax.broadcasted_iota(jnp.int32, (ATT_QBLK, ATT_KBLK), 1)
        lc, jc = l // CHUNK, j // CHUNK
        band = (jc >= lc) & (jc <= lc + ATT_LEFT_CHUNKS)
        for h in range(hb):
            src = jnp.broadcast_to(rv_ref[h], (ATT_QBLK, ATT_RVEC))
            toe = pltpu.roll(src, ATT_RVEC - ATT_QBLK, 1, stride=1, stride_axis=0)[:, :ATT_KBLK]
            for v in range(ATT_VARIANTS):
                ok = band & (j >= (ATT_KBLK - ATT_QBLK) - ATT_QBLK * v)
                b = jnp.where(ok, toe, NEG)
                for half in range(2):
                    bias_sc[v, h, half] = b[half * ATT_HALF:(half + 1) * ATT_HALF,
                                            half * ATT_HALF:half * ATT_HALF + ATT_WIN]

    v = jnp.minimum(i, ATT_VARIANTS - 1)
    units = [(h, half) for h in range(hb) for half in range(2)]

    def lanes(h):
        return slice(h * ATT_HEADDIM, (h + 1) * ATT_HEADDIM)

    def rows(half):
        return slice(half * ATT_HALF, (half + 1) * ATT_HALF)

    def window(refs, h, half):
        cat = jnp.concatenate([r[0, :, lanes(h)] for r in refs], axis=0)
        return cat[half * ATT_HALF:half * ATT_HALF + ATT_WIN]

    def scores(h, half):
        s = lax.dot_general(q_ref[0, rows(half), lanes(h)], window((k0_ref, k1_ref, k2_ref), h, half),
                            (((1,), (1,)), ((), ())), preferred_element_type=F32)
        return s + bias_sc[v, h, half]

    def probs(s):
        p = jnp.exp2(s - jnp.max(s, axis=1, keepdims=True))
        return p.astype(BF16), jnp.sum(p, axis=1, keepdims=True)

    def finish(h, half, p, den):
        o = jnp.dot(p, window((v0_ref, v1_ref, v2_ref), h, half), preferred_element_type=F32) / den
        gate = _silu_of_twice(z_ref[0, rows(half), lanes(h)].astype(F32))
        o_ref[0, rows(half), lanes(h)] = (o * gate).astype(o_ref.dtype)

    stages = [lambda u, _: scores(*units[u]), lambda u, s: probs(s), lambda u, pd: finish(*units[u], *pd)]
    inflight = {}
    n = len(units)
    for step in range(n + len(stages) - 1):
        for k, stage in enumerate(stages):
            u = step - k
            if 0 <= u < n:
                inflight[u] = stage(u, inflight.get(u))


def _rel_source_rows(rel_table):
    u = np.arange(ATT_RVEC)
    bucket = np.clip(ATT_KBLK - u, -REL_MAX, REL_MAX) + REL_MAX
    return (rel_table[:, bucket].astype(F32) * LOG2E)[:, None, :]


def _attention(qz, kv, rel_rows, hb=16):
    bsz, s, two_d = qz.shape
    d = two_d // 2
    hw = hb * ATT_HEADDIM
    ng = d // hw
    cur = lambda off: pl.BlockSpec((1, ATT_QBLK, hw), lambda hg, b, i: (b, i, off + hg))
    back = lambda off, n: pl.BlockSpec((1, ATT_QBLK, hw), lambda hg, b, i: (b, jnp.maximum(i - n, 0), off + hg))
    return pl.pallas_call(
        functools.partial(_attn_kernel, hb=hb),
        grid=(ng, bsz, s // ATT_QBLK),
        in_specs=[cur(0), cur(ng), back(0, 2), back(0, 1), cur(0), back(ng, 2), back(ng, 1), cur(ng),
                  pl.BlockSpec((hb, 1, ATT_RVEC), lambda hg, b, i: (hg, 0, 0))],
        out_specs=pl.BlockSpec((1, ATT_QBLK, hw), lambda hg, b, i: (b, i, hg)),
        out_shape=jax.ShapeDtypeStruct((bsz, s, d), BF16),
        scratch_shapes=[pltpu.VMEM((ATT_VARIANTS, hb, 2, ATT_HALF, ATT_WIN), F32)],
        compiler_params=_params(3), name="band_attention",
    )(qz, qz, kv, kv, kv, kv, kv, kv, rel_rows)


def kernel(x, c, ada_w, ada_b, pre_norm_g, post_norm_g, ssm_w_in, ssm_conv_w, ssm_conv_b, ssm_dt_bias, ssm_a_log, ssm_d, ssm_norm_g, ssm_w_out, kv_norm_g, w_kv, att_w_in, att_rel_bias, att_w_out):
    bsz, s, d = x.shape
    depth = ada_w.shape[0]
    n_a = ssm_w_in.shape[0]
    t = bsz * s
    d_inner = ssm_w_out.shape[1]
    nheads = ssm_dt_bias.shape[1]
    zx_cols = ssm_w_in.shape[2] - nheads

    mod = _ada_modulation(c, ada_w, ada_b)
    shift = [mod[l, :, None, 0:d] for l in range(depth)]
    scale = [mod[l, :, None, d:2 * d] for l in range(depth)]
    gate = [mod[l, :, None, 2 * d:3 * d] for l in range(depth)]

    att_d = att_w_out.shape[1]
    qscale = jnp.concatenate([jnp.full((att_d,), ATT_HEADDIM ** -0.5 * LOG2E, F32), jnp.full((att_d,), 0.5, F32)])
    zscale = jnp.concatenate([jnp.full((d_inner,), 0.5, F32), jnp.ones((ssm_w_in.shape[2] - d_inner,), F32)])
    n_b = depth - n_a
    w_in_b = (ssm_w_in[0:1] * zscale).astype(BF16)
    w_out_b = w_kv_b = None
    att_in_b, att_out_b = [None] * n_b, [None] * n_b

    (h,) = _resnorm(x, pre=(shift[0], scale[0], pre_norm_g[0]))
    kv = None
    for layer in range(depth):
        h2 = h.reshape(t, d)
        if layer < n_a:
            i = layer
            jobs = [_CastJob(ssm_w_out, i, None)]
            if i + 1 < n_a:
                jobs.append(_CastJob(ssm_w_in, i + 1, zscale))
            else:
                jobs += [_CastJob(w_kv[None], 0, None), _CastJob(att_w_in, 0, qscale)]
            zx, w_out_b, *rest = _ssm_in_proj(h2, w_in_b, 0, ssm_conv_w[i], ssm_conv_b[i], d_inner, s, jobs=jobs)
            dt = _matmul(h2, w_in_b, 0, F32, 1024, LANES, "ssm_dt_proj", col0=zx_cols, ncols=nheads,
                         bias=ssm_dt_bias[i])
            if i + 1 < n_a:
                w_in_b = rest[0][None]
            else:
                w_kv_b, att_in_b[0] = rest
            yn = _ssd_scan(zx.reshape(bsz, s, zx_cols), dt.reshape(bsz, s, nheads), ssm_a_log[i], ssm_d[i],
                           ssm_norm_g[i])
            y = _matmul(yn.reshape(t, d_inner), w_out_b[None], 0, BF16, 1024, 512, "ssm_out_proj")
        else:
            i = layer - n_a
            qz = _matmul(h2, att_in_b[i][None], 0, BF16, 1024, 1024, "att_in_proj")
            o = _attention(qz.reshape(bsz, s, 2 * att_d), kv, _rel_source_rows(att_rel_bias[i]))
            y = _matmul(o.reshape(t, att_d), att_out_b[i][None], 0, BF16, 1024, 1024, "att_out_proj")
        post = (y.reshape(bsz, s, d), gate[layer], post_norm_g[layer])
        if layer + 1 < depth:
            nxt = layer + 1
            pre = (shift[nxt], scale[nxt], pre_norm_g[nxt])
            if nxt == n_a:
                x, h, hkv = _resnorm(x, post=post, pre=pre, kv_g=kv_norm_g)
                jobs = [_CastJob(att_w_out, 0, None)]
                for bi in range(1, n_b):
                    jobs += [_CastJob(att_w_in, bi, qscale), _CastJob(att_w_out, bi, None)]
                kv, att_out_b[0], *rest = _matmul(hkv.reshape(t, d), w_kv_b[None], 0, BF16, 1024, 1024, "kv_proj",
                                                  jobs=jobs)
                att_in_b[1:], att_out_b[1:] = rest[0::2], rest[1::2]
                kv = kv.reshape(bsz, s, w_kv.shape[1])
            else:
                x, h = _resnorm(x, post=post, pre=pre)
        else:
            (x,) = _resnorm(x, post=post)
    return x
```

```python
import functools
from typing import NamedTuple, Optional

import jax
import jax.numpy as jnp
import numpy as np
from jax import lax
from jax.experimental import pallas as pl
from jax.experimental.pallas import tpu as pltpu

F32 = jnp.float32
BF16 = jnp.bfloat16

EPS = 1e-6
CHUNK = 64
LANES = 128
MXU_DIM = 256
V7X_VMEM_BYTES = 64 * 1024 * 1024
VMEM_LIMIT = V7X_VMEM_BYTES - 8 * 1024 * 1024

SSM_HEADDIM = 64
SSM_GROUPS = 8
SSM_STATE = 128
SSM_CONV = 4
ATT_HEADDIM = 128
ATT_LEFT_CHUNKS = 8
REL_MAX = 128
ATT_QBLK = 4 * CHUNK
ATT_KBLK = 3 * ATT_QBLK
ATT_HALF = ATT_QBLK // 2
ATT_WIN = ATT_KBLK - ATT_HALF
ATT_RVEC = 1024
ATT_VARIANTS = 3
LOG2E = 1.4426950408889634
NEG = -1e30
SIDE_JOB_BLOCKS = 256


def _params(n_grid, vmem=VMEM_LIMIT):
    return pltpu.CompilerParams(dimension_semantics=("arbitrary",) * n_grid, vmem_limit_bytes=vmem)


def _silu_of_twice(h):
    return h + h * jnp.tanh(h)


def _silu(v):
    return _silu_of_twice(0.5 * v)


def _softplus(v):
    return jnp.maximum(v, 0.0) + jnp.log1p(jnp.exp(-jnp.abs(v)))


def _ada_kernel(ct_ref, w_ref, b_ref, o_ref, sb_ref, *, nb, tn):
    d = ct_ref.shape[0]

    @pl.when((pl.program_id(0) == 0) & (pl.program_id(1) == 0))
    def _():
        s = _silu(ct_ref[...])
        for b in range(nb):
            sb_ref[b] = jnp.broadcast_to(s[:, b:b + 1], (d, LANES))

    rows = []
    for b in range(nb):
        cols = []
        for t in range(tn // LANES):
            w = w_ref[0, :, t * LANES:(t + 1) * LANES]
            cols.append(jnp.sum(w * sb_ref[b], axis=0, keepdims=True))
        rows.append(jnp.concatenate(cols, axis=1))
    o_ref[0] = jnp.concatenate(rows, axis=0) + b_ref[0]


def _ada_modulation(c, ada_w, ada_b, tn=512):
    depth, d, n3 = ada_w.shape
    nb = c.shape[0]
    return pl.pallas_call(
        functools.partial(_ada_kernel, nb=nb, tn=tn),
        grid=(depth, n3 // tn),
        in_specs=[pl.BlockSpec((d, nb), lambda l, j: (0, 0)),
                  pl.BlockSpec((1, d, tn), lambda l, j: (l, 0, j)),
                  pl.BlockSpec((1, 1, tn), lambda l, j: (l, 0, j))],
        out_specs=pl.BlockSpec((1, nb, tn), lambda l, j: (l, 0, j)),
        out_shape=jax.ShapeDtypeStruct((depth, nb, n3), F32),
        scratch_shapes=[pltpu.VMEM((nb, d, LANES), F32)],
        compiler_params=_params(2),
        name="ada_modulation",
    )(c.T, ada_w, ada_b.reshape(depth, 1, n3))


def _rms(v, g):
    var = jnp.mean(v * v, axis=-1, keepdims=True)
    return v * lax.rsqrt(var + EPS) * g


def _resnorm_kernel(*refs, has_post, has_pre, has_kv):
    it = iter(refs)
    x_ref = next(it)
    if has_post:
        y_ref, gate_ref, gpost_ref = next(it), next(it), next(it)
    if has_pre:
        shift_ref, scale_ref, gpre_ref = next(it), next(it), next(it)
    if has_kv:
        gkv_ref = next(it)
    x = x_ref[0]
    if has_post:
        x = x + gate_ref[0] * _rms(y_ref[0].astype(F32), gpost_ref[...])
        next(it)[0] = x
    if has_pre:
        h = _rms(x, gpre_ref[...]) * (1.0 + scale_ref[0]) + shift_ref[0]
        next(it)[0] = h.astype(BF16)
    if has_kv:
        next(it)[0] = _rms(x, gkv_ref[...]).astype(BF16)


def _resnorm(x, post=None, pre=None, kv_g=None, tr=256):
    bsz, s, d = x.shape
    row = pl.BlockSpec((1, tr, d), lambda b, i: (b, i, 0))
    per_b = pl.BlockSpec((1, 1, d), lambda b, i: (b, 0, 0))
    vec = pl.BlockSpec((1, d), lambda b, i: (0, 0))
    args, in_specs, out_specs, out_shape = [x], [row], [], []
    if post is not None:
        y, gate, g_post = post
        args += [y, gate, g_post.reshape(1, d)]
        in_specs += [row, per_b, vec]
        out_specs.append(row)
        out_shape.append(jax.ShapeDtypeStruct((bsz, s, d), F32))
    if pre is not None:
        shift, scale, g_pre = pre
        args += [shift, scale, g_pre.reshape(1, d)]
        in_specs += [per_b, per_b, vec]
        out_specs.append(row)
        out_shape.append(jax.ShapeDtypeStruct((bsz, s, d), BF16))
    if kv_g is not None:
        args.append(kv_g.reshape(1, d))
        in_specs.append(vec)
        out_specs.append(row)
        out_shape.append(jax.ShapeDtypeStruct((bsz, s, d), BF16))
    return pl.pallas_call(
        functools.partial(_resnorm_kernel, has_post=post is not None, has_pre=pre is not None,
                          has_kv=kv_g is not None),
        grid=(bsz, s // tr), in_specs=in_specs, out_specs=out_specs, out_shape=out_shape,
        compiler_params=_params(2), name="resnorm",
    )(*args)


class _CastJob(NamedTuple):
    w: jax.Array
    layer: int
    scale: Optional[jax.Array]


def _job_specs(jobs, ni, nj):
    nblk = min(SIDE_JOB_BLOCKS, 1 << ((ni * nj).bit_length() - 1))
    args, in_specs, out_specs, out_shape = [], [], [], []
    for job in jobs:
        _, r, c = job.w.shape
        rb = r // nblk
        blk = lambda i, j: jnp.minimum(i * nj + j, nblk - 1)
        args.append(job.w)
        in_specs.append(pl.BlockSpec((None, rb, c), lambda i, j, l=job.layer: (l, blk(i, j), 0)))
        if job.scale is not None:
            args.append(job.scale.reshape(1, c))
            in_specs.append(pl.BlockSpec((1, c), lambda i, j: (0, 0)))
        out_specs.append(pl.BlockSpec((rb, c), lambda i, j: (blk(i, j), 0)))
        out_shape.append(jax.ShapeDtypeStruct((r, c), BF16))
    return args, in_specs, out_specs, out_shape


def _run_cast_jobs(in_refs, out_refs, scaled):
    it = iter(in_refs)
    for o_ref, has_scale in zip(out_refs, scaled):
        w = next(it)[...]
        if has_scale:
            w = w * next(it)[...]
        o_ref[...] = w.astype(o_ref.dtype)


def _n_job_inputs(scaled):
    return len(scaled) + sum(scaled)


def _mm_kernel(*refs, has_bias, scaled):
    n_in = 2 + has_bias + _n_job_inputs(scaled)
    a_ref, b_ref = refs[:2]
    o_ref = refs[n_in]
    _run_cast_jobs(refs[2 + has_bias:n_in], refs[n_in + 1:], scaled)
    tn = o_ref.shape[1]
    sub = min(MXU_DIM, tn)
    for c in range(tn // sub):
        cs = slice(c * sub, (c + 1) * sub)
        acc = jnp.dot(a_ref[...], b_ref[:, cs], preferred_element_type=F32)
        if has_bias:
            acc = _softplus(acc + refs[2][:, cs])
        o_ref[:, cs] = acc.astype(o_ref.dtype)


def _matmul(a, b, layer, out_dtype, tm, tn, name, col0=0, ncols=None, bias=None, jobs=()):
    m, k = a.shape
    n = b.shape[2] - col0 if ncols is None else ncols
    tm, tn = min(tm, m), min(tn, n)
    jb0 = col0 // tn
    args = [a, b]
    in_specs = [pl.BlockSpec((tm, k), lambda i, j: (i, 0)),
                pl.BlockSpec((None, k, tn), lambda i, j: (layer, 0, jb0 + j))]
    if bias is not None:
        args.append(bias.reshape(1, n))
        in_specs.append(pl.BlockSpec((1, tn), lambda i, j: (0, j)))
    j_args, j_in, j_out, j_shape = _job_specs(jobs, m // tm, n // tn)
    out = pl.pallas_call(
        functools.partial(_mm_kernel, has_bias=bias is not None, scaled=tuple(jb.scale is not None for jb in jobs)),
        grid=(m // tm, n // tn),
        in_specs=in_specs + j_in,
        out_specs=[pl.BlockSpec((tm, tn), lambda i, j: (i, j))] + j_out,
        out_shape=[jax.ShapeDtypeStruct((m, n), out_dtype)] + j_shape,
        compiler_params=_params(2), name=name,
    )(*args, *j_args)
    return out if jobs else out[0]


CONV_HALO = 8


def _ssm_in_kernel(*refs, n_plain, tiles_per_seq, scaled):
    n_in = 4 + _n_job_inputs(scaled)
    a_ref, w_ref, cw_ref, cb_ref = refs[:4]
    o_ref, tail_ref = refs[n_in], refs[-1]
    _run_cast_jobs(refs[4:n_in], refs[n_in + 1:-1], scaled)
    i, j = pl.program_id(0), pl.program_id(1)
    tm, tn = o_ref.shape
    sub = min(MXU_DIM, tn)

    @pl.when((i == 0) & (j == 0))
    def _():
        tail_ref[...] = jnp.zeros_like(tail_ref)

    def cols(c):
        return slice(c * sub, (c + 1) * sub)

    def sub_dots(epilogue):
        for c in range(tn // sub):
            epilogue(jnp.dot(a_ref[...], w_ref[:, cols(c)], preferred_element_type=F32), cols(c))

    @pl.when(j < n_plain)
    def _():
        def plain(acc, cs):
            o_ref[:, cs] = acc.astype(o_ref.dtype)
        sub_dots(plain)

    @pl.when(j >= n_plain)
    def _():
        jc = j - n_plain
        seq_start = i % tiles_per_seq == 0
        nt = tm // CONV_HALO
        row = lax.broadcasted_iota(jnp.int32, (nt, CONV_HALO, sub), 1)

        def delayed(v, before, d):
            r = pltpu.roll(v, d, 1)
            prev = jnp.concatenate([pltpu.roll(before, d, 0)[None], r[:-1]], axis=0)
            return jnp.where(row < d, prev, r)

        def conv(acc, cs):
            tail = jnp.where(seq_start, 0.0, tail_ref[jc, :, cs])
            tail_ref[jc, :, cs] = acc[tm - CONV_HALO:]
            w = cw_ref[:, cs]
            w0, w1, w2, w3 = (w[k:k + 1] for k in range(SSM_CONV))
            a = acc.reshape(nt, CONV_HALO, sub)
            a1 = delayed(a, tail, 1)
            pair = w1 * a + w0 * a1
            pair_tail = w1 * tail + w0 * pltpu.roll(tail, 1, 0)
            half = cb_ref[:, cs] + w3 * a + w2 * a1 + delayed(pair, pair_tail, 2)
            o_ref[:, cs] = (half + half * jnp.tanh(half)).reshape(tm, sub).astype(o_ref.dtype)
        sub_dots(conv)


def _ssm_in_proj(h2, w_in, layer, conv_w, conv_b, n_plain_cols, seq, jobs=(), tm=1024, tn=1024):
    t, k = h2.shape
    c = conv_w.shape[1]
    n = n_plain_cols + c
    n_plain = n_plain_cols // tn
    conv_col = lambda i, j: (0, jnp.maximum(j - n_plain, 0))
    j_args, j_in, j_out, j_shape = _job_specs(jobs, t // tm, n // tn)
    return pl.pallas_call(
        functools.partial(_ssm_in_kernel, n_plain=n_plain, tiles_per_seq=seq // tm,
                          scaled=tuple(jb.scale is not None for jb in jobs)),
        grid=(t // tm, n // tn),
        in_specs=[pl.BlockSpec((tm, k), lambda i, j: (i, 0)),
                  pl.BlockSpec((None, k, tn), lambda i, j: (layer, 0, j)),
                  pl.BlockSpec((SSM_CONV, tn), conv_col),
                  pl.BlockSpec((1, tn), conv_col)] + j_in,
        out_specs=[pl.BlockSpec((tm, tn), lambda i, j: (i, j))] + j_out,
        out_shape=[jax.ShapeDtypeStruct((t, n), BF16)] + j_shape,
        scratch_shapes=[pltpu.VMEM((c // tn, CONV_HALO, tn), F32)],
        compiler_params=_params(2), name="ssm_in_proj",
    )(h2, w_in, 0.5 * conv_w, 0.5 * conv_b.reshape(1, c), *j_args)


def _scan_kernel(x_ref, b_ref, c_ref, z_ref, dta_ref, dtt_ref, alr_ref, alc_ref,
                 dexp_ref, ng_ref, ex_ref, o_ref, st_ref, *, nsub):
    L = CHUNK
    P = SSM_HEADDIM
    hp = LANES // P
    nheads = alr_ref.shape[2] // nsub
    npair = nheads // hp
    qh = MXU_DIM // P
    nquad = nheads // qh

    @pl.when(pl.program_id(2) == 0)
    def _():
        st_ref[...] = jnp.zeros_like(st_ref)

    a_r = -LOG2E * jnp.exp(alr_ref[0])
    a_c = -LOG2E * jnp.exp(alc_ref[0])
    lane = lax.broadcasted_iota(jnp.int32, (L, LANES), 1)
    sub = lax.broadcasted_iota(jnp.int32, (L, LANES), 0)
    lo = lane < P
    s_of_lane = lane & (P - 1)
    causal2 = s_of_lane <= sub
    tri = (lax.broadcasted_iota(jnp.int32, (L, L), 1) <= lax.broadcasted_iota(jnp.int32, (L, L), 0)).astype(BF16)
    tri_t2 = (sub <= s_of_lane).astype(BF16)
    blk = lax.broadcasted_iota(jnp.int32, (L, qh * P), 1) // P
    dexp = dexp_ref[...]
    ng = ng_ref[...]

    def pieces(v, n=3):
        out, rem = [], v
        for _ in range(n):
            hi = rem.astype(BF16).astype(F32)
            out.append(hi)
            rem = rem - hi
        return out

    dt_all = dta_ref[0, 0, 0]
    dt_t_all = dtt_ref[0, 0, 0]
    acum_all = sum(jnp.dot(tri, p.astype(BF16), preferred_element_type=F32) for p in pieces(dt_all * a_r))
    acum_t_all = sum(jnp.dot(p.astype(BF16), tri_t2, preferred_element_type=F32)
                     for p in pieces(dt_t_all[:, :L] * a_c))
    row_all = acum_t_all - jnp.log2(dt_t_all)
    e_out_all = jnp.exp2(acum_all[L - 1:L, :] - acum_all) * dt_all
    acum_pieces, e_out_pieces = pieces(acum_all, 3), pieces(e_out_all, 2)
    ex = ex_ref[...]

    def spread(pcs, ci):
        lhs = jnp.concatenate([p[:, ci * nheads:(ci + 1) * nheads] for p in pcs], axis=1).astype(BF16)
        return jnp.dot(lhs, ex[:len(pcs) * nheads], preferred_element_type=F32)

    def local(ci):
        rows = slice(ci * L, (ci + 1) * L)
        x = x_ref[0, rows, :]
        bm = b_ref[0, rows, :]
        cm = c_ref[0, rows, :]
        row_t = row_all[ci * nheads:(ci + 1) * nheads]
        col, sc_out = spread(acum_pieces, ci), spread(e_out_pieces, ci)
        sc_in = jnp.exp2(col)
        cb2 = lax.dot_general(cm, jnp.concatenate([bm, bm], axis=0), (((1,), (1,)), ((), ())),
                              preferred_element_type=F32)
        xf = x.astype(F32)

        w_tiles = []
        for j in range(npair):
            r0, r1 = hp * j, hp * j + 1
            row = jnp.where(lo[0:1], row_t[r0:r0 + 1], row_t[r1:r1 + 1])
            seg = jnp.where(causal2, col[:, j * LANES:(j + 1) * LANES] - row, NEG)
            w_tiles.append((jnp.exp2(seg) * cb2).astype(BF16))

        y_parts = []
        for q in range(nquad):
            w4 = jnp.concatenate(w_tiles[q * qh // hp:(q + 1) * qh // hp], axis=1)
            x4 = x[:, q * qh * P:(q + 1) * qh * P]
            xbd = jnp.concatenate([jnp.where(blk == i, x4, jnp.zeros_like(x4)) for i in range(qh)], axis=0)
            y_parts.append(jnp.dot(w4, xbd, preferred_element_type=F32))
        y_loc = jnp.concatenate(y_parts, axis=1) + xf * dexp

        xw = x * sc_out.astype(BF16)
        upd = lax.dot_general(bm, xw, (((0,), (0,)), ((), ())), preferred_element_type=F32)
        return cm, sc_in, y_loc, upd

    def carried(ci, st, cm, sc_in, y_loc, upd):
        rows = slice(ci * L, (ci + 1) * L)
        y = y_loc + jnp.dot(cm, st.astype(BF16), preferred_element_type=F32) * sc_in
        yg = y * _silu_of_twice(z_ref[0, rows, :].astype(F32))
        o_ref[0, rows, :] = _rms(yg, ng).astype(o_ref.dtype)
        return st * sc_in[L - 1:L, :] + upd

    st = st_ref[...]
    for ci in range(nsub):
        st = carried(ci, st, *local(ci))
    st_ref[...] = st


def _spread_matrix(nheads_g, parts=3):
    rows = np.arange(parts * nheads_g)
    cols = np.arange(nheads_g * SSM_HEADDIM)
    return jnp.asarray((rows % nheads_g)[:, None] == (cols // SSM_HEADDIM)[None, :], dtype=BF16)


def _ssd_scan(zx, dt, a_log, d_skip, norm_g, nsub=16):
    bsz, s, _ = zx.shape
    g = SSM_GROUPS
    nheads = dt.shape[-1]
    r = nheads // g
    gw = r * SSM_HEADDIM
    d_inner = nheads * SSM_HEADDIM
    nc = s // CHUNK
    lb = nsub * CHUNK
    nstep = nc // nsub
    dt6 = dt.reshape(bsz, nstep, nsub, CHUNK, g, r)
    dt_a = dt6.transpose(0, 4, 1, 3, 2, 5).reshape(bsz, g, nstep, CHUNK, nsub * r)
    dt_t = dt6.transpose(0, 4, 1, 2, 5, 3).reshape(bsz, g, nstep, nsub * r, CHUNK)
    dt_t = jnp.concatenate([dt_t, dt_t], axis=-1)
    x_off = d_inner // gw
    b_off = 2 * d_inner // SSM_STATE
    spread = _spread_matrix(r)
    row_spec = pl.BlockSpec((1, 1, nsub * r), lambda b, gi, ci: (gi, 0, 0))
    col_spec = pl.BlockSpec((1, nsub * r, 1), lambda b, gi, ci: (gi, 0, 0))
    a_log_steps = jnp.tile(a_log.reshape(g, 1, r), (1, nsub, 1))
    wide = lambda off: pl.BlockSpec((1, lb, gw), lambda b, gi, ci: (b, ci, off + gi))
    narrow = lambda off: pl.BlockSpec((1, lb, SSM_STATE), lambda b, gi, ci: (b, ci, off + gi))
    return pl.pallas_call(
        functools.partial(_scan_kernel, nsub=nsub),
        grid=(bsz, g, nstep),
        in_specs=[wide(x_off), narrow(b_off), narrow(b_off + g), wide(0),
                  pl.BlockSpec((1, 1, 1, CHUNK, nsub * r), lambda b, gi, ci: (b, gi, ci, 0, 0)),
                  pl.BlockSpec((1, 1, 1, nsub * r, 2 * CHUNK), lambda b, gi, ci: (b, gi, ci, 0, 0)),
                  row_spec, col_spec,
                  pl.BlockSpec((1, gw), lambda b, gi, ci: (0, gi)),
                  pl.BlockSpec((1, gw), lambda b, gi, ci: (0, gi)),
                  pl.BlockSpec(spread.shape, lambda b, gi, ci: (0, 0))],
        out_specs=wide(0),
        out_shape=jax.ShapeDtypeStruct((bsz, s, d_inner), BF16),
        scratch_shapes=[pltpu.VMEM((SSM_STATE, gw), F32)],
        compiler_params=_params(3), name="ssd_scan",
    )(zx, zx, zx, zx, dt_a, dt_t, a_log_steps.reshape(g, 1, nsub * r), a_log_steps.reshape(g, nsub * r, 1),
      jnp.repeat(d_skip, SSM_HEADDIM).reshape(1, d_inner), norm_g.reshape(1, d_inner), spread)


def _attn_kernel(q_ref, z_ref, k0_ref, k1_ref, k2_ref, v0_ref, v1_ref, v2_ref, rv_ref, o_ref, bias_sc, *, hb):
    i = pl.program_id(2)

    @pl.when((pl.program_id(1) == 0) & (i == 0))
    def _():
        l = lax.broadcasted_iota(jnp.int32, (ATT_QBLK, ATT_KBLK), 0)
        j = lax.broadcasted_iota(jnp.int32, (ATT_QBLK, ATT_KBLK), 1)
        lc, jc = l // CHUNK, j // CHUNK
        band = (jc >= lc) & (jc <= lc + ATT_LEFT_CHUNKS)
        for h in range(hb):
            src = jnp.broadcast_to(rv_ref[h], (ATT_QBLK, ATT_RVEC))
            toe = pltpu.roll(src, ATT_RVEC - ATT_QBLK, 1, stride=1, stride_axis=0)[:, :ATT_KBLK]
            for v in range(ATT_VARIANTS):
                ok = band & (j >= (ATT_KBLK - ATT_QBLK) - ATT_QBLK * v)
                b = jnp.where(ok, toe, NEG)
                for half in range(2):
                    bias_sc[v, h, half] = b[half * ATT_HALF:(half + 1) * ATT_HALF,
                                            half * ATT_HALF:half * ATT_HALF + ATT_WIN]

    v = jnp.minimum(i, ATT_VARIANTS - 1)
    units = [(h, half) for h in range(hb) for half in range(2)]

    def lanes(h):
        return slice(h * ATT_HEADDIM, (h + 1) * ATT_HEADDIM)

    def rows(half):
        return slice(half * ATT_HALF, (half + 1) * ATT_HALF)

    def window(refs, h, half):
        cat = jnp.concatenate([r[0, :, lanes(h)] for r in refs], axis=0)
        return cat[half * ATT_HALF:half * ATT_HALF + ATT_WIN]

    def scores(h, half):
        s = lax.dot_general(q_ref[0, rows(half), lanes(h)], window((k0_ref, k1_ref, k2_ref), h, half),
                            (((1,), (1,)), ((), ())), preferred_element_type=F32)
        return s + bias_sc[v, h, half]

    def probs(s):
        p = jnp.exp2(s - jnp.max(s, axis=1, keepdims=True))
        return p.astype(BF16), jnp.sum(p, axis=1, keepdims=True)

    def finish(h, half, p, den):
        o = jnp.dot(p, window((v0_ref, v1_ref, v2_ref), h, half), preferred_element_type=F32) / den
        gate = _silu_of_twice(z_ref[0, rows(half), lanes(h)].astype(F32))
        o_ref[0, rows(half), lanes(h)] = (o * gate).astype(o_ref.dtype)

    stages = [lambda u, _: scores(*units[u]), lambda u, s: probs(s), lambda u, pd: finish(*units[u], *pd)]
    inflight = {}
    n = len(units)
    for step in range(n + len(stages) - 1):
        for k, stage in enumerate(stages):
            u = step - k
            if 0 <= u < n:
                inflight[u] = stage(u, inflight.get(u))


def _rel_source_rows(rel_table):
    u = np.arange(ATT_RVEC)
    bucket = np.clip(ATT_KBLK - u, -REL_MAX, REL_MAX) + REL_MAX
    return (rel_table[:, bucket].astype(F32) * LOG2E)[:, None, :]


def _attention(qz, kv, rel_rows, hb=16):
    bsz, s, two_d = qz.shape
    d = two_d // 2
    hw = hb * ATT_HEADDIM
    ng = d // hw
    cur = lambda off: pl.BlockSpec((1, ATT_QBLK, hw), lambda hg, b, i: (b, i, off + hg))
    back = lambda off, n: pl.BlockSpec((1, ATT_QBLK, hw), lambda hg, b, i: (b, jnp.maximum(i - n, 0), off + hg))
    return pl.pallas_call(
        functools.partial(_attn_kernel, hb=hb),
        grid=(ng, bsz, s // ATT_QBLK),
        in_specs=[cur(0), cur(ng), back(0, 2), back(0, 1), cur(0), back(ng, 2), back(ng, 1), cur(ng),
                  pl.BlockSpec((hb, 1, ATT_RVEC), lambda hg, b, i: (hg, 0, 0))],
        out_specs=pl.BlockSpec((1, ATT_QBLK, hw), lambda hg, b, i: (b, i, hg)),
        out_shape=jax.ShapeDtypeStruct((bsz, s, d), BF16),
        scratch_shapes=[pltpu.VMEM((ATT_VARIANTS, hb, 2, ATT_HALF, ATT_WIN), F32)],
        compiler_params=_params(3), name="band_attention",
    )(qz, qz, kv, kv, kv, kv, kv, kv, rel_rows)


def kernel(x, c, ada_w, ada_b, pre_norm_g, post_norm_g, ssm_w_in, ssm_conv_w, ssm_conv_b, ssm_dt_bias, ssm_a_log, ssm_d, ssm_norm_g, ssm_w_out, kv_norm_g, w_kv, att_w_in, att_rel_bias, att_w_out):
    bsz, s, d = x.shape
    depth = ada_w.shape[0]
    n_a = ssm_w_in.shape[0]
    t = bsz * s
    d_inner = ssm_w_out.shape[1]
    nheads = ssm_dt_bias.shape[1]
    zx_cols = ssm_w_in.shape[2] - nheads

    mod = _ada_modulation(c, ada_w, ada_b)
    shift = [mod[l, :, None, 0:d] for l in range(depth)]
    scale = [mod[l, :, None, d:2 * d] for l in range(depth)]
    gate = [mod[l, :, None, 2 * d:3 * d] for l in range(depth)]

    att_d = att_w_out.shape[1]
    qscale = jnp.concatenate([jnp.full((att_d,), ATT_HEADDIM ** -0.5 * LOG2E, F32), jnp.full((att_d,), 0.5, F32)])
    zscale = jnp.concatenate([jnp.full((d_inner,), 0.5, F32), jnp.ones((ssm_w_in.shape[2] - d_inner,), F32)])
    n_b = depth - n_a
    w_in_b = (ssm_w_in[0:1] * zscale).astype(BF16)
    w_out_b = w_kv_b = None
    att_in_b, att_out_b = [None] * n_b, [None] * n_b

    (h,) = _resnorm(x, pre=(shift[0], scale[0], pre_norm_g[0]))
    kv = None
    for layer in range(depth):
        h2 = h.reshape(t, d)
        if layer < n_a:
            i = layer
            jobs = [_CastJob(ssm_w_out, i, None)]
            if i + 1 < n_a:
                jobs.append(_CastJob(ssm_w_in, i + 1, zscale))
            else:
                jobs += [_CastJob(w_kv[None], 0, None), _CastJob(att_w_in, 0, qscale)]
            zx, w_out_b, *rest = _ssm_in_proj(h2, w_in_b, 0, ssm_conv_w[i], ssm_conv_b[i], d_inner, s, jobs=jobs)
            dt = _matmul(h2, w_in_b, 0, F32, 1024, LANES, "ssm_dt_proj", col0=zx_cols, ncols=nheads,
                         bias=ssm_dt_bias[i])
            if i + 1 < n_a:
                w_in_b = rest[0][None]
            else:
                w_kv_b, att_in_b[0] = rest
            yn = _ssd_scan(zx.reshape(bsz, s, zx_cols), dt.reshape(bsz, s, nheads), ssm_a_log[i], ssm_d[i],
                           ssm_norm_g[i])
            y = _matmul(yn.reshape(t, d_inner), w_out_b[None], 0, BF16, 1024, 512, "ssm_out_proj")
        else:
            i = layer - n_a
            qz = _matmul(h2, att_in_b[i][None], 0, BF16, 1024, 1024, "att_in_proj")
            o = _attention(qz.reshape(bsz, s, 2 * att_d), kv, _rel_source_rows(att_rel_bias[i]))
            y = _matmul(o.reshape(t, att_d), att_out_b[i][None], 0, BF16, 1024, 1024, "att_out_proj")
        post = (y.reshape(bsz, s, d), gate[layer], post_norm_g[layer])
        if layer + 1 < depth:
            nxt = layer + 1
            pre = (shift[nxt], scale[nxt], pre_norm_g[nxt])
            if nxt == n_a:
                x, h, hkv = _resnorm(x, post=post, pre=pre, kv_g=kv_norm_g)
                jobs = [_CastJob(att_w_out, 0, None)]
                for bi in range(1, n_b):
                    jobs += [_CastJob(att_w_in, bi, qscale), _CastJob(att_w_out, bi, None)]
                kv, att_out_b[0], *rest = _matmul(hkv.reshape(t, d), w_kv_b[None], 0, BF16, 1024, 1024, "kv_proj",
                                                  jobs=jobs)
                att_in_b[1:], att_out_b[1:] = rest[0::2], rest[1::2]
                kv = kv.reshape(bsz, s, w_kv.shape[1])
            else:
                x, h = _resnorm(x, post=post, pre=pre)
        else:
            (x,) = _resnorm(x, post=post)
    return x
```

```python
import functools
from typing import NamedTuple, Optional

import jax
import jax.numpy as jnp
import numpy as np
from jax import lax
from jax.experimental import pallas as pl
from jax.experimental.pallas import tpu as pltpu

F32 = jnp.float32
BF16 = jnp.bfloat16

EPS = 1e-6
CHUNK = 64
LANES = 128
MXU_DIM = 256
V7X_VMEM_BYTES = 64 * 1024 * 1024
VMEM_LIMIT = V7X_VMEM_BYTES - 8 * 1024 * 1024

SSM_HEADDIM = 64
SSM_GROUPS = 8
SSM_STATE = 128
SSM_CONV = 4
ATT_HEADDIM = 128
ATT_LEFT_CHUNKS = 8
REL_MAX = 128
ATT_QBLK = 4 * CHUNK
ATT_KBLK = 3 * ATT_QBLK
ATT_HALF = ATT_QBLK // 2
ATT_WIN = ATT_KBLK - ATT_HALF
ATT_RVEC = 1024
ATT_VARIANTS = 3
LOG2E = 1.4426950408889634
NEG = -1e30
SIDE_JOB_BLOCKS = 256


def _params(n_grid, vmem=VMEM_LIMIT):
    return pltpu.CompilerParams(dimension_semantics=("arbitrary",) * n_grid, vmem_limit_bytes=vmem)


def _silu_of_twice(h):
    return h + h * jnp.tanh(h)


def _silu(v):
    return _silu_of_twice(0.5 * v)


def _softplus(v):
    return jnp.maximum(v, 0.0) + jnp.log1p(jnp.exp(-jnp.abs(v)))


def _ada_kernel(ct_ref, w_ref, b_ref, o_ref, sb_ref, *, nb, tn):
    d = ct_ref.shape[0]

    @pl.when((pl.program_id(0) == 0) & (pl.program_id(1) == 0))
    def _():
        s = _silu(ct_ref[...])
        for b in range(nb):
            sb_ref[b] = jnp.broadcast_to(s[:, b:b + 1], (d, LANES))

    rows = []
    for b in range(nb):
        cols = []
        for t in range(tn // LANES):
            w = w_ref[0, :, t * LANES:(t + 1) * LANES]
            cols.append(jnp.sum(w * sb_ref[b], axis=0, keepdims=True))
        rows.append(jnp.concatenate(cols, axis=1))
    o_ref[0] = jnp.concatenate(rows, axis=0) + b_ref[0]


def _ada_modulation(c, ada_w, ada_b, tn=512):
    depth, d, n3 = ada_w.shape
    nb = c.shape[0]
    return pl.pallas_call(
        functools.partial(_ada_kernel, nb=nb, tn=tn),
        grid=(depth, n3 // tn),
        in_specs=[pl.BlockSpec((d, nb), lambda l, j: (0, 0)),
                  pl.BlockSpec((1, d, tn), lambda l, j: (l, 0, j)),
                  pl.BlockSpec((1, 1, tn), lambda l, j: (l, 0, j))],
        out_specs=pl.BlockSpec((1, nb, tn), lambda l, j: (l, 0, j)),
        out_shape=jax.ShapeDtypeStruct((depth, nb, n3), F32),
        scratch_shapes=[pltpu.VMEM((nb, d, LANES), F32)],
        compiler_params=_params(2),
        name="ada_modulation",
    )(c.T, ada_w, ada_b.reshape(depth, 1, n3))


def _rms(v, g):
    var = jnp.mean(v * v, axis=-1, keepdims=True)
    return v * lax.rsqrt(var + EPS) * g


def _resnorm_kernel(*refs, has_post, has_pre, has_kv, has_dt):
    it = iter(refs)
    x_ref = next(it)
    if has_post:
        y_ref, gate_ref, gpost_ref = next(it), next(it), next(it)
    if has_pre:
        shift_ref, scale_ref, gpre_ref = next(it), next(it), next(it)
    if has_kv:
        gkv_ref = next(it)
    if has_dt:
        wdt_ref, bdt_ref = next(it), next(it)
    x = x_ref[0]
    if has_post:
        x = x + gate_ref[0] * _rms(y_ref[0].astype(F32), gpost_ref[...])
        next(it)[0] = x
    if has_pre:
        h = (_rms(x, gpre_ref[...]) * (1.0 + scale_ref[0]) + shift_ref[0]).astype(BF16)
        next(it)[0] = h
    if has_kv:
        next(it)[0] = _rms(x, gkv_ref[...]).astype(BF16)
    if has_dt:
        next(it)[0] = _softplus(jnp.dot(h, wdt_ref[...], preferred_element_type=F32) + bdt_ref[...])


def _resnorm(x, post=None, pre=None, kv_g=None, dt=None, tr=256):
    bsz, s, d = x.shape
    row = pl.BlockSpec((1, tr, d), lambda b, i: (b, i, 0))
    per_b = pl.BlockSpec((1, 1, d), lambda b, i: (b, 0, 0))
    vec = pl.BlockSpec((1, d), lambda b, i: (0, 0))
    args, in_specs, out_specs, out_shape = [x], [row], [], []
    if post is not None:
        y, gate, g_post = post
        args += [y, gate, g_post.reshape(1, d)]
        in_specs += [row, per_b, vec]
        out_specs.append(row)
        out_shape.append(jax.ShapeDtypeStruct((bsz, s, d), F32))
    if pre is not None:
        shift, scale, g_pre = pre
        args += [shift, scale, g_pre.reshape(1, d)]
        in_specs += [per_b, per_b, vec]
        out_specs.append(row)
        out_shape.append(jax.ShapeDtypeStruct((bsz, s, d), BF16))
    if kv_g is not None:
        args.append(kv_g.reshape(1, d))
        in_specs.append(vec)
        out_specs.append(row)
        out_shape.append(jax.ShapeDtypeStruct((bsz, s, d), BF16))
    if dt is not None:
        w, layer, col0, dt_bias = dt
        nh = dt_bias.shape[0]
        args += [w, dt_bias.reshape(1, nh)]
        in_specs += [pl.BlockSpec((None, d, nh), lambda b, i: (layer, 0, col0 // nh)),
                     pl.BlockSpec((1, nh), lambda b, i: (0, 0))]
        out_specs.append(pl.BlockSpec((1, tr, nh), lambda b, i: (b, i, 0)))
        out_shape.append(jax.ShapeDtypeStruct((bsz, s, nh), F32))
    return pl.pallas_call(
        functools.partial(_resnorm_kernel, has_post=post is not None, has_pre=pre is not None,
                          has_kv=kv_g is not None, has_dt=dt is not None),
        grid=(bsz, s // tr), in_specs=in_specs, out_specs=out_specs, out_shape=out_shape,
        compiler_params=_params(2), name="resnorm",
    )(*args)


class _CastJob(NamedTuple):
    w: jax.Array
    layer: int
    scale: Optional[jax.Array]


def _job_specs(jobs, ni, nj):
    nblk = min(SIDE_JOB_BLOCKS, 1 << ((ni * nj).bit_length() - 1))
    args, in_specs, out_specs, out_shape = [], [], [], []
    for job in jobs:
        _, r, c = job.w.shape
        rb = r // nblk
        blk = lambda i, j: jnp.minimum(i * nj + j, nblk - 1)
        args.append(job.w)
        in_specs.append(pl.BlockSpec((None, rb, c), lambda i, j, l=job.layer: (l, blk(i, j), 0)))
        if job.scale is not None:
            args.append(job.scale.reshape(1, c))
            in_specs.append(pl.BlockSpec((1, c), lambda i, j: (0, 0)))
        out_specs.append(pl.BlockSpec((rb, c), lambda i, j: (blk(i, j), 0)))
        out_shape.append(jax.ShapeDtypeStruct((r, c), BF16))
    return args, in_specs, out_specs, out_shape


def _run_cast_jobs(in_refs, out_refs, scaled):
    it = iter(in_refs)
    for o_ref, has_scale in zip(out_refs, scaled):
        w = next(it)[...]
        if has_scale:
            w = w * next(it)[...]
        o_ref[...] = w.astype(o_ref.dtype)


def _n_job_inputs(scaled):
    return len(scaled) + sum(scaled)


def _mm_kernel(*refs, scaled):
    n_in = 2 + _n_job_inputs(scaled)
    a_ref, b_ref = refs[:2]
    o_ref = refs[n_in]
    _run_cast_jobs(refs[2:n_in], refs[n_in + 1:], scaled)
    tn = o_ref.shape[1]
    sub = min(MXU_DIM, tn)
    for c in range(tn // sub):
        cs = slice(c * sub, (c + 1) * sub)
        o_ref[:, cs] = jnp.dot(a_ref[...], b_ref[:, cs], preferred_element_type=F32).astype(o_ref.dtype)


def _matmul(a, b, layer, out_dtype, tm, tn, name, jobs=()):
    m, k = a.shape
    n = b.shape[2]
    tm, tn = min(tm, m), min(tn, n)
    j_args, j_in, j_out, j_shape = _job_specs(jobs, m // tm, n // tn)
    out = pl.pallas_call(
        functools.partial(_mm_kernel, scaled=tuple(jb.scale is not None for jb in jobs)),
        grid=(m // tm, n // tn),
        in_specs=[pl.BlockSpec((tm, k), lambda i, j: (i, 0)),
                  pl.BlockSpec((None, k, tn), lambda i, j: (layer, 0, j))] + j_in,
        out_specs=[pl.BlockSpec((tm, tn), lambda i, j: (i, j))] + j_out,
        out_shape=[jax.ShapeDtypeStruct((m, n), out_dtype)] + j_shape,
        compiler_params=_params(2), name=name,
    )(a, b, *j_args)
    return out if jobs else out[0]


CONV_HALO = 8


def _ssm_in_kernel(*refs, n_plain, tiles_per_seq, scaled):
    n_in = 4 + _n_job_inputs(scaled)
    a_ref, w_ref, cw_ref, cb_ref = refs[:4]
    o_ref, tail_ref = refs[n_in], refs[-1]
    _run_cast_jobs(refs[4:n_in], refs[n_in + 1:-1], scaled)
    i, j = pl.program_id(0), pl.program_id(1)
    tm, tn = o_ref.shape
    sub = min(MXU_DIM, tn)

    @pl.when((i == 0) & (j == 0))
    def _():
        tail_ref[...] = jnp.zeros_like(tail_ref)

    def cols(c):
        return slice(c * sub, (c + 1) * sub)

    def sub_dots(epilogue):
        for c in range(tn // sub):
            epilogue(jnp.dot(a_ref[...], w_ref[:, cols(c)], preferred_element_type=F32), cols(c))

    @pl.when(j < n_plain)
    def _():
        def plain(acc, cs):
            o_ref[:, cs] = acc.astype(o_ref.dtype)
        sub_dots(plain)

    @pl.when(j >= n_plain)
    def _():
        jc = j - n_plain
        seq_start = i % tiles_per_seq == 0
        nt = tm // CONV_HALO
        row = lax.broadcasted_iota(jnp.int32, (nt, CONV_HALO, sub), 1)

        def delayed(v, before, d):
            r = pltpu.roll(v, d, 1)
            prev = jnp.concatenate([pltpu.roll(before, d, 0)[None], r[:-1]], axis=0)
            return jnp.where(row < d, prev, r)

        def conv(acc, cs):
            tail = jnp.where(seq_start, 0.0, tail_ref[jc, :, cs])
            tail_ref[jc, :, cs] = acc[tm - CONV_HALO:]
            w = cw_ref[:, cs]
            w0, w1, w2, w3 = (w[k:k + 1] for k in range(SSM_CONV))
            a = acc.reshape(nt, CONV_HALO, sub)
            a1 = delayed(a, tail, 1)
            pair = w1 * a + w0 * a1
            pair_tail = w1 * tail + w0 * pltpu.roll(tail, 1, 0)
            half = cb_ref[:, cs] + w3 * a + w2 * a1 + delayed(pair, pair_tail, 2)
            o_ref[:, cs] = (half + half * jnp.tanh(half)).reshape(tm, sub).astype(o_ref.dtype)
        sub_dots(conv)


def _ssm_in_proj(h2, w_in, layer, conv_w, conv_b, n_plain_cols, seq, jobs=(), tm=1024, tn=1024):
    t, k = h2.shape
    c = conv_w.shape[1]
    n = n_plain_cols + c
    n_plain = n_plain_cols // tn
    conv_col = lambda i, j: (0, jnp.maximum(j - n_plain, 0))
    j_args, j_in, j_out, j_shape = _job_specs(jobs, t // tm, n // tn)
    return pl.pallas_call(
        functools.partial(_ssm_in_kernel, n_plain=n_plain, tiles_per_seq=seq // tm,
                          scaled=tuple(jb.scale is not None for jb in jobs)),
        grid=(t // tm, n // tn),
        in_specs=[pl.BlockSpec((tm, k), lambda i, j: (i, 0)),
                  pl.BlockSpec((None, k, tn), lambda i, j: (layer, 0, j)),
                  pl.BlockSpec((SSM_CONV, tn), conv_col),
                  pl.BlockSpec((1, tn), conv_col)] + j_in,
        out_specs=[pl.BlockSpec((tm, tn), lambda i, j: (i, j))] + j_out,
        out_shape=[jax.ShapeDtypeStruct((t, n), BF16)] + j_shape,
        scratch_shapes=[pltpu.VMEM((c // tn, CONV_HALO, tn), F32)],
        compiler_params=_params(2), name="ssm_in_proj",
    )(h2, w_in, 0.5 * conv_w, 0.5 * conv_b.reshape(1, c), *j_args)


def _scan_kernel(x_ref, b_ref, c_ref, z_ref, dta_ref, dtt_ref, alr_ref, alc_ref,
                 dexp_ref, ng_ref, ex_ref, o_ref, st_ref, *, nsub):
    L = CHUNK
    P = SSM_HEADDIM
    hp = LANES // P
    nheads = alr_ref.shape[2] // nsub
    npair = nheads // hp
    qh = MXU_DIM // P
    nquad = nheads // qh

    @pl.when(pl.program_id(2) == 0)
    def _():
        st_ref[...] = jnp.zeros_like(st_ref)

    a_r = -LOG2E * jnp.exp(alr_ref[0])
    a_c = -LOG2E * jnp.exp(alc_ref[0])
    lane = lax.broadcasted_iota(jnp.int32, (L, LANES), 1)
    sub = lax.broadcasted_iota(jnp.int32, (L, LANES), 0)
    lo = lane < P
    s_of_lane = lane & (P - 1)
    causal2 = s_of_lane <= sub
    tri = (lax.broadcasted_iota(jnp.int32, (L, L), 1) <= lax.broadcasted_iota(jnp.int32, (L, L), 0)).astype(BF16)
    tri_t2 = (sub <= s_of_lane).astype(BF16)
    blk = lax.broadcasted_iota(jnp.int32, (L, qh * P), 1) // P
    dexp = dexp_ref[...]
    ng = ng_ref[...]

    def pieces(v, n=3):
        out, rem = [], v
        for _ in range(n):
            hi = rem.astype(BF16).astype(F32)
            out.append(hi)
            rem = rem - hi
        return out

    dt_all = dta_ref[0, 0, 0]
    dt_t_all = dtt_ref[0, 0, 0]
    acum_all = sum(jnp.dot(tri, p.astype(BF16), preferred_element_type=F32) for p in pieces(dt_all * a_r))
    acum_t_all = sum(jnp.dot(p.astype(BF16), tri_t2, preferred_element_type=F32)
                     for p in pieces(dt_t_all[:, :L] * a_c))
    row_all = acum_t_all - jnp.log2(dt_t_all)
    e_out_all = jnp.exp2(acum_all[L - 1:L, :] - acum_all) * dt_all
    acum_pieces, e_out_pieces = pieces(acum_all, 3), pieces(e_out_all, 2)
    ex = ex_ref[...]

    def spread(pcs, ci):
        lhs = jnp.concatenate([p[:, ci * nheads:(ci + 1) * nheads] for p in pcs], axis=1).astype(BF16)
        return jnp.dot(lhs, ex[:len(pcs) * nheads], preferred_element_type=F32)

    def local(ci):
        rows = slice(ci * L, (ci + 1) * L)
        x = x_ref[0, rows, :]
        bm = b_ref[0, rows, :]
        cm = c_ref[0, rows, :]
        row_t = row_all[ci * nheads:(ci + 1) * nheads]
        col, sc_out = spread(acum_pieces, ci), spread(e_out_pieces, ci)
        sc_in = jnp.exp2(col)
        cb2 = lax.dot_general(cm, jnp.concatenate([bm, bm], axis=0), (((1,), (1,)), ((), ())),
                              preferred_element_type=F32)
        xf = x.astype(F32)

        w_tiles = []
        for j in range(npair):
            r0, r1 = hp * j, hp * j + 1
            row = jnp.where(lo[0:1], row_t[r0:r0 + 1], row_t[r1:r1 + 1])
            seg = jnp.where(causal2, col[:, j * LANES:(j + 1) * LANES] - row, NEG)
            w_tiles.append((jnp.exp2(seg) * cb2).astype(BF16))

        y_parts = []
        for q in range(nquad):
            w4 = jnp.concatenate(w_tiles[q * qh // hp:(q + 1) * qh // hp], axis=1)
            x4 = x[:, q * qh * P:(q + 1) * qh * P]
            xbd = jnp.concatenate([jnp.where(blk == i, x4, jnp.zeros_like(x4)) for i in range(qh)], axis=0)
            y_parts.append(jnp.dot(w4, xbd, preferred_element_type=F32))
        y_loc = jnp.concatenate(y_parts, axis=1) + xf * dexp

        xw = x * sc_out.astype(BF16)
        upd = lax.dot_general(bm, xw, (((0,), (0,)), ((), ())), preferred_element_type=F32)
        return cm, sc_in, y_loc, upd

    def carried(ci, st, cm, sc_in, y_loc, upd):
        rows = slice(ci * L, (ci + 1) * L)
        y = y_loc + jnp.dot(cm, st.astype(BF16), preferred_element_type=F32) * sc_in
        yg = y * _silu_of_twice(z_ref[0, rows, :].astype(F32))
        o_ref[0, rows, :] = _rms(yg, ng).astype(o_ref.dtype)
        return st * sc_in[L - 1:L, :] + upd

    st = st_ref[...]
    for ci in range(nsub):
        st = carried(ci, st, *local(ci))
    st_ref[...] = st


def _spread_matrix(nheads_g, parts=3):
    rows = np.arange(parts * nheads_g)
    cols = np.arange(nheads_g * SSM_HEADDIM)
    return jnp.asarray((rows % nheads_g)[:, None] == (cols // SSM_HEADDIM)[None, :], dtype=BF16)


def _ssd_scan(zx, dt, a_log, d_skip, norm_g, nsub=16):
    bsz, s, _ = zx.shape
    g = SSM_GROUPS
    nheads = dt.shape[-1]
    r = nheads // g
    gw = r * SSM_HEADDIM
    d_inner = nheads * SSM_HEADDIM
    nc = s // CHUNK
    lb = nsub * CHUNK
    nstep = nc // nsub
    dt6 = dt.reshape(bsz, nstep, nsub, CHUNK, g, r)
    dt_a = dt6.transpose(0, 4, 1, 3, 2, 5).reshape(bsz, g, nstep, CHUNK, nsub * r)
    dt_t = dt6.transpose(0, 4, 1, 2, 5, 3).reshape(bsz, g, nstep, nsub * r, CHUNK)
    dt_t = jnp.concatenate([dt_t, dt_t], axis=-1)
    x_off = d_inner // gw
    b_off = 2 * d_inner // SSM_STATE
    spread = _spread_matrix(r)
    row_spec = pl.BlockSpec((1, 1, nsub * r), lambda b, gi, ci: (gi, 0, 0))
    col_spec = pl.BlockSpec((1, nsub * r, 1), lambda b, gi, ci: (gi, 0, 0))
    a_log_steps = jnp.tile(a_log.reshape(g, 1, r), (1, nsub, 1))
    wide = lambda off: pl.BlockSpec((1, lb, gw), lambda b, gi, ci: (b, ci, off + gi))
    narrow = lambda off: pl.BlockSpec((1, lb, SSM_STATE), lambda b, gi, ci: (b, ci, off + gi))
    return pl.pallas_call(
        functools.partial(_scan_kernel, nsub=nsub),
        grid=(bsz, g, nstep),
        in_specs=[wide(x_off), narrow(b_off), narrow(b_off + g), wide(0),
                  pl.BlockSpec((1, 1, 1, CHUNK, nsub * r), lambda b, gi, ci: (b, gi, ci, 0, 0)),
                  pl.BlockSpec((1, 1, 1, nsub * r, 2 * CHUNK), lambda b, gi, ci: (b, gi, ci, 0, 0)),
                  row_spec, col_spec,
                  pl.BlockSpec((1, gw), lambda b, gi, ci: (0, gi)),
                  pl.BlockSpec((1, gw), lambda b, gi, ci: (0, gi)),
                  pl.BlockSpec(spread.shape, lambda b, gi, ci: (0, 0))],
        out_specs=wide(0),
        out_shape=jax.ShapeDtypeStruct((bsz, s, d_inner), BF16),
        scratch_shapes=[pltpu.VMEM((SSM_STATE, gw), F32)],
        compiler_params=_params(3), name="ssd_scan",
    )(zx, zx, zx, zx, dt_a, dt_t, a_log_steps.reshape(g, 1, nsub * r), a_log_steps.reshape(g, nsub * r, 1),
      jnp.repeat(d_skip, SSM_HEADDIM).reshape(1, d_inner), norm_g.reshape(1, d_inner), spread)


def _attn_kernel(q_ref, z_ref, k0_ref, k1_ref, k2_ref, v0_ref, v1_ref, v2_ref, rv_ref, o_ref, bias_sc, *, hb):
    i = pl.program_id(2)

    @pl.when((pl.program_id(1) == 0) & (i == 0))
    def _():
        l = lax.broadcasted_iota(jnp.int32, (ATT_QBLK, ATT_KBLK), 0)
        j = lax.broadcasted_iota(jnp.int32, (ATT_QBLK, ATT_KBLK), 1)
        lc, jc = l // CHUNK, j // CHUNK
        band = (jc >= lc) & (jc <= lc + ATT_LEFT_CHUNKS)
        for h in range(hb):
            src = jnp.broadcast_to(rv_ref[h], (ATT_QBLK, ATT_RVEC))
            toe = pltpu.roll(src, ATT_RVEC - ATT_QBLK, 1, stride=1, stride_axis=0)[:, :ATT_KBLK]
            for v in range(ATT_VARIANTS):
                ok = band & (j >= (ATT_KBLK - ATT_QBLK) - ATT_QBLK * v)
                b = jnp.where(ok, toe, NEG)
                for half in range(2):
                    bias_sc[v, h, half] = b[half * ATT_HALF:(half + 1) * ATT_HALF,
                                            half * ATT_HALF:half * ATT_HALF + ATT_WIN]

    v = jnp.minimum(i, ATT_VARIANTS - 1)
    units = [(h, half) for h in range(hb) for half in range(2)]

    def lanes(h):
        return slice(h * ATT_HEADDIM, (h + 1) * ATT_HEADDIM)

    def rows(half):
        return slice(half * ATT_HALF, (half + 1) * ATT_HALF)

    def window(refs, h, half):
        cat = jnp.concatenate([r[0, :, lanes(h)] for r in refs], axis=0)
        return cat[half * ATT_HALF:half * ATT_HALF + ATT_WIN]

    def scores(h, half):
        s = lax.dot_general(q_ref[0, rows(half), lanes(h)], window((k0_ref, k1_ref, k2_ref), h, half),
                            (((1,), (1,)), ((), ())), preferred_element_type=F32)
        return s + bias_sc[v, h, half]

    def probs(s):
        p = jnp.exp2(s - jnp.max(s, axis=1, keepdims=True))
        return p.astype(BF16), jnp.sum(p, axis=1, keepdims=True)

    def finish(h, half, p, den):
        o = jnp.dot(p, window((v0_ref, v1_ref, v2_ref), h, half), preferred_element_type=F32) / den
        gate = _silu_of_twice(z_ref[0, rows(half), lanes(h)].astype(F32))
        o_ref[0, rows(half), lanes(h)] = (o * gate).astype(o_ref.dtype)

    stages = [lambda u, _: scores(*units[u]), lambda u, s: probs(s), lambda u, pd: finish(*units[u], *pd)]
    inflight = {}
    n = len(units)
    for step in range(n + len(stages) - 1):
        for k, stage in enumerate(stages):
            u = step - k
            if 0 <= u < n:
                inflight[u] = stage(u, inflight.get(u))


def _rel_source_rows(rel_table):
    u = np.arange(ATT_RVEC)
    bucket = np.clip(ATT_KBLK - u, -REL_MAX, REL_MAX) + REL_MAX
    return (rel_table[:, bucket].astype(F32) * LOG2E)[:, None, :]


def _attention(qz, kv, rel_rows, hb=16):
    bsz, s, two_d = qz.shape
    d = two_d // 2
    hw = hb * ATT_HEADDIM
    ng = d // hw
    cur = lambda off: pl.BlockSpec((1, ATT_QBLK, hw), lambda hg, b, i: (b, i, off + hg))
    back = lambda off, n: pl.BlockSpec((1, ATT_QBLK, hw), lambda hg, b, i: (b, jnp.maximum(i - n, 0), off + hg))
    return pl.pallas_call(
        functools.partial(_attn_kernel, hb=hb),
        grid=(ng, bsz, s // ATT_QBLK),
        in_specs=[cur(0), cur(ng), back(0, 2), back(0, 1), cur(0), back(ng, 2), back(ng, 1), cur(ng),
                  pl.BlockSpec((hb, 1, ATT_RVEC), lambda hg, b, i: (hg, 0, 0))],
        out_specs=pl.BlockSpec((1, ATT_QBLK, hw), lambda hg, b, i: (b, i, hg)),
        out_shape=jax.ShapeDtypeStruct((bsz, s, d), BF16),
        scratch_shapes=[pltpu.VMEM((ATT_VARIANTS, hb, 2, ATT_HALF, ATT_WIN), F32)],
        compiler_params=_params(3), name="band_attention",
    )(qz, qz, kv, kv, kv, kv, kv, kv, rel_rows)


def kernel(x, c, ada_w, ada_b, pre_norm_g, post_norm_g, ssm_w_in, ssm_conv_w, ssm_conv_b, ssm_dt_bias, ssm_a_log, ssm_d, ssm_norm_g, ssm_w_out, kv_norm_g, w_kv, att_w_in, att_rel_bias, att_w_out):
    bsz, s, d = x.shape
    depth = ada_w.shape[0]
    n_a = ssm_w_in.shape[0]
    t = bsz * s
    d_inner = ssm_w_out.shape[1]
    nheads = ssm_dt_bias.shape[1]
    zx_cols = ssm_w_in.shape[2] - nheads

    mod = _ada_modulation(c, ada_w, ada_b)
    shift = [mod[l, :, None, 0:d] for l in range(depth)]
    scale = [mod[l, :, None, d:2 * d] for l in range(depth)]
    gate = [mod[l, :, None, 2 * d:3 * d] for l in range(depth)]

    att_d = att_w_out.shape[1]
    qscale = jnp.concatenate([jnp.full((att_d,), ATT_HEADDIM ** -0.5 * LOG2E, F32), jnp.full((att_d,), 0.5, F32)])
    zscale = jnp.concatenate([jnp.full((d_inner,), 0.5, F32), jnp.ones((ssm_w_in.shape[2] - d_inner,), F32)])
    n_b = depth - n_a
    w_in_b = (ssm_w_in[0:1] * zscale).astype(BF16)
    w_out_b = w_kv_b = None
    att_in_b, att_out_b = [None] * n_b, [None] * n_b

    dt_of = lambda i: (w_in_b, 0, zx_cols, ssm_dt_bias[i])
    h, *dt = _resnorm(x, pre=(shift[0], scale[0], pre_norm_g[0]), dt=dt_of(0) if n_a > 0 else None)
    kv = None
    for layer in range(depth):
        h2 = h.reshape(t, d)
        if layer < n_a:
            i = layer
            jobs = [_CastJob(ssm_w_out, i, None)]
            if i + 1 < n_a:
                jobs.append(_CastJob(ssm_w_in, i + 1, zscale))
            else:
                jobs += [_CastJob(w_kv[None], 0, None), _CastJob(att_w_in, 0, qscale)]
            zx, w_out_b, *rest = _ssm_in_proj(h2, w_in_b, 0, ssm_conv_w[i], ssm_conv_b[i], d_inner, s, jobs=jobs)
            if i + 1 < n_a:
                w_in_b = rest[0][None]
            else:
                w_kv_b, att_in_b[0] = rest
            yn = _ssd_scan(zx.reshape(bsz, s, zx_cols), dt[0], ssm_a_log[i], ssm_d[i], ssm_norm_g[i])
            y = _matmul(yn.reshape(t, d_inner), w_out_b[None], 0, BF16, 1024, 512, "ssm_out_proj")
        else:
            i = layer - n_a
            qz = _matmul(h2, att_in_b[i][None], 0, BF16, 1024, 1024, "att_in_proj")
            o = _attention(qz.reshape(bsz, s, 2 * att_d), kv, _rel_source_rows(att_rel_bias[i]))
            y = _matmul(o.reshape(t, att_d), att_out_b[i][None], 0, BF16, 1024, 1024, "att_out_proj")
        post = (y.reshape(bsz, s, d), gate[layer], post_norm_g[layer])
        if layer + 1 < depth:
            nxt = layer + 1
            pre = (shift[nxt], scale[nxt], pre_norm_g[nxt])
            if nxt == n_a:
                x, h, hkv = _resnorm(x, post=post, pre=pre, kv_g=kv_norm_g)
                jobs = [_CastJob(att_w_out, 0, None)]
                for bi in range(1, n_b):
                    jobs += [_CastJob(att_w_in, bi, qscale), _CastJob(att_w_out, bi, None)]
                kv, att_out_b[0], *rest = _matmul(hkv.reshape(t, d), w_kv_b[None], 0, BF16, 1024, 1024, "kv_proj",
                                                  jobs=jobs)
                att_in_b[1:], att_out_b[1:] = rest[0::2], rest[1::2]
                kv = kv.reshape(bsz, s, w_kv.shape[1])
            else:
                x, h, *dt = _resnorm(x, post=post, pre=pre, dt=dt_of(nxt) if nxt < n_a else None)
        else:
            (x,) = _resnorm(x, post=post)
    return x
```
